```python
import jax, jax.numpy as jnp
from jax import lax
import numpy as np

D_MODEL = 4096
BATCH = 2
SEQ = 8192
DEPTH = 1

RWKV_HEAD_DIM = 64
D_RWKV = D_MODEL // 2
RWKV_HEADS = D_RWKV // RWKV_HEAD_DIM
LORA_DECAY = 96
LORA_ICLR = 96
LORA_GATE = 256
D_CONV = D_MODEL // 2
CONV_WIDTH = 3
D_FF = 11008
NORM_EPS = 1e-6
LNX_EPS = 64e-5
N_ADA = 6
N_SHIFT = 3 * D_RWKV + LORA_DECAY + LORA_ICLR + LORA_GATE
N_IN = N_SHIFT + 3 * D_CONV + 2 * D_MODEL

kernel_name = "hybrid_rwkv7_shortconv_convffn_adaln"


def rmsnorm(x, gain):
    xf = x.astype(jnp.float32)
    y = xf * lax.rsqrt(jnp.mean(xf * xf, axis=-1, keepdims=True) + NORM_EPS)
    return (y * gain.astype(jnp.float32)).astype(x.dtype)


def modulate(h, shift, scale):
    return h * (1.0 + scale[:, None, :]) + shift[:, None, :]


def causal_dwconv(x, w):
    seq = x.shape[1]
    xp = jnp.pad(x, ((0, 0), (CONV_WIDTH - 1, 0), (0, 0)))
    return sum(xp[:, k:k + seq, :] * w[k] for k in range(CONV_WIDTH))


def token_shift(p, mu):
    prev = jnp.pad(p, ((0, 0), (1, 0), (0, 0)))[:, :-1, :]
    return p + (prev - p) * mu


def rwkv7_scan(r, decay, k, v, a, b):
    bsz, _, h, n = r.shape

    def step(S, inp):
        r_t, w_t, k_t, v_t, a_t, b_t = inp
        sa = jnp.einsum('bhij,bhj->bhi', S, a_t)
        S = (S * w_t[:, :, None, :] + sa[..., None] * b_t[:, :, None, :]
             + v_t[..., None] * k_t[:, :, None, :])
        return S, jnp.einsum('bhij,bhj->bhi', S, r_t)

    xs = tuple(jnp.moveaxis(t, 1, 0) for t in (r, decay, k, v, a, b))
    S0 = jnp.zeros((bsz, h, n, n), jnp.float32)
    _, ys = lax.scan(step, S0, xs)
    return jnp.moveaxis(ys, 0, 1)


def rwkv7_time_mix(p_r, p_k, p_v, p_wd, p_ad, p_gd, w0, a0, k_k, k_a, r_k,
                   w_lora_decay, w_lora_iclr, w_lora_gate, lnx_w, lnx_b):
    f32 = jnp.float32
    bsz, seq, _ = p_r.shape

    def heads(t):
        return t.astype(f32).reshape(bsz, seq, RWKV_HEADS, RWKV_HEAD_DIM)

    w_log = -jax.nn.softplus(-(w0 + jnp.tanh(p_wd) @ w_lora_decay).astype(f32)) - 0.5
    decay = jnp.exp(-jnp.exp(w_log))
    iclr = jax.nn.sigmoid((a0 + p_ad @ w_lora_iclr).astype(f32))
    g = (jax.nn.sigmoid(p_gd) @ w_lora_gate).astype(f32)
    kk = heads(p_k * k_k)
    kk = kk / jnp.maximum(jnp.sqrt(jnp.sum(kk * kk, axis=-1, keepdims=True)), 1e-12)
    k = p_k.astype(f32) * (1.0 + (iclr - 1.0) * k_a.astype(f32))
    r_h, k_h, v_h = heads(p_r), heads(k), heads(p_v)
    y = rwkv7_scan(r_h, heads(decay), k_h, v_h, -kk, kk * heads(iclr))
    mean = jnp.mean(y, axis=-1, keepdims=True)
    var = jnp.mean(jnp.square(y - mean), axis=-1, keepdims=True)
    y = ((y - mean) * lax.rsqrt(var + LNX_EPS)).reshape(bsz, seq, D_RWKV)
    y = y * lnx_w.astype(f32) + lnx_b.astype(f32)
    bonus = jnp.sum(r_h * k_h * r_k.astype(f32), axis=-1, keepdims=True) * v_h
    o = (y + bonus.reshape(bsz, seq, D_RWKV)) * g
    return o.astype(p_r.dtype)


def token_mixer(h, w_in, mu_shift, w0, a0, k_k, k_a, r_k, w_lora_decay, w_lora_iclr,
                w_lora_gate, lnx_w, lnx_b, conv_w_mix, w_o_rwkv, w_o_conv, w_out):
    proj = h @ w_in
    p_shift = token_shift(proj[..., :N_SHIFT], mu_shift)
    p_conv = proj[..., N_SHIFT:N_SHIFT + 3 * D_CONV]
    p_gate = proj[..., N_SHIFT + 3 * D_CONV:]
    cuts = np.cumsum([D_RWKV, D_RWKV, D_RWKV, LORA_DECAY, LORA_ICLR]).tolist()
    p_r, p_k, p_v, p_wd, p_ad, p_gd = jnp.split(p_shift, cuts, axis=-1)
    y_a = rwkv7_time_mix(p_r, p_k, p_v, p_wd, p_ad, p_gd, w0, a0, k_k, k_a, r_k,
                         w_lora_decay, w_lora_iclr, w_lora_gate, lnx_w, lnx_b) @ w_o_rwkv
    c_b, c_c, c_x = jnp.split(p_conv, 3, axis=-1)
    y_b = (c_b * causal_dwconv(c_c * c_x, conv_w_mix)) @ w_o_conv
    g_a, g_b = jnp.split(p_gate, 2, axis=-1)
    merged = jax.nn.sigmoid(g_a) * y_a + jax.nn.sigmoid(g_b) * y_b
    return merged @ w_out


def channel_mixer(h, w_ffn_up, conv_w_ffn, w_ffn_down):
    u = h @ w_ffn_up
    gate, val = u[..., :D_FF], u[..., D_FF:]
    gate = causal_dwconv(gate, conv_w_ffn)
    return (jax.nn.silu(gate) * val) @ w_ffn_down


def setup_inputs(seed: int = 0) -> dict:
    key = jax.random.key(seed)
    ks = jax.random.split(key, 32)
    f32 = jnp.float32
    nrm = lambda k, shape, s: jax.random.normal(k, shape, f32) * s
    L = DEPTH
    return {
        "x": nrm(ks[0], (BATCH, SEQ, D_MODEL), 1.0),
        "c": nrm(ks[1], (BATCH, D_MODEL), 1.0),
        "w_ada": nrm(ks[2], (L, D_MODEL, N_ADA * D_MODEL), 0.5 * D_MODEL ** -0.5),
        "b_ada": nrm(ks[3], (L, N_ADA * D_MODEL), 0.02),
        "norm1_gain": 1.0 + nrm(ks[4], (L, D_MODEL), 0.02),
        "w_in": nrm(ks[5], (L, D_MODEL, N_IN), D_MODEL ** -0.5),
        "mu_shift": jax.random.uniform(ks[6], (L, N_SHIFT), f32),
        "w0": jax.random.uniform(ks[7], (L, D_RWKV), f32, -6.0, -1.0),
        "a0": nrm(ks[8], (L, D_RWKV), 0.1),
        "k_k": 0.85 + nrm(ks[9], (L, D_RWKV), 0.05),
        "k_a": 1.0 + nrm(ks[10], (L, D_RWKV), 0.05),
        "r_k": nrm(ks[11], (L, RWKV_HEADS, RWKV_HEAD_DIM), 0.1),
        "w_lora_decay": nrm(ks[12], (L, LORA_DECAY, D_RWKV), 0.5 * LORA_DECAY ** -0.5),
        "w_lora_iclr": nrm(ks[13], (L, LORA_ICLR, D_RWKV), 0.5 * LORA_ICLR ** -0.5),
        "w_lora_gate": nrm(ks[14], (L, LORA_GATE, D_RWKV), LORA_GATE ** -0.5),
        "lnx_w": 1.0 + nrm(ks[15], (L, D_RWKV), 0.02),
        "lnx_b": nrm(ks[16], (L, D_RWKV), 0.02),
        "conv_w_mix": nrm(ks[17], (L, CONV_WIDTH, D_CONV), CONV_WIDTH ** -0.5),
        "w_o_rwkv": nrm(ks[18], (L, D_RWKV, D_MODEL), D_RWKV ** -0.5),
        "w_o_conv": nrm(ks[19], (L, D_CONV, D_MODEL), D_CONV ** -0.5),
        "w_out": nrm(ks[20], (L, D_MODEL, D_MODEL), D_MODEL ** -0.5),
        "norm2_gain": 1.0 + nrm(ks[21], (L, D_MODEL), 0.02),
        "w_ffn_up": nrm(ks[22], (L, D_MODEL, 2 * D_FF), D_MODEL ** -0.5),
        "conv_w_ffn": nrm(ks[23], (L, CONV_WIDTH, D_FF), CONV_WIDTH ** -0.5),
        "w_ffn_down": nrm(ks[24], (L, D_FF, D_MODEL), D_FF ** -0.5),
        "final_gain": 1.0 + nrm(ks[25], (D_MODEL,), 0.02),
    }


def reference(x, c, w_ada, b_ada, norm1_gain, w_in, mu_shift, w0, a0, k_k, k_a, r_k,
              w_lora_decay, w_lora_iclr, w_lora_gate, lnx_w, lnx_b, conv_w_mix,
              w_o_rwkv, w_o_conv, w_out, norm2_gain, w_ffn_up, conv_w_ffn, w_ffn_down,
              final_gain):
    c_act = jax.nn.silu(c)
    for layer in range(DEPTH):
        mod = c_act @ w_ada[layer] + b_ada[layer]
        shift1, scale1, gate1, shift2, scale2, gate2 = jnp.split(mod, N_ADA, axis=-1)
        h = modulate(rmsnorm(x, norm1_gain[layer]), shift1, scale1)
        y = token_mixer(h, w_in[layer], mu_shift[layer], w0[layer], a0[layer], k_k[layer],
                        k_a[layer], r_k[layer], w_lora_decay[layer], w_lora_iclr[layer],
                        w_lora_gate[layer], lnx_w[layer], lnx_b[layer], conv_w_mix[layer],
                        w_o_rwkv[layer], w_o_conv[layer], w_out[layer])
        x = x + gate1[:, None, :] * y
        h = modulate(rmsnorm(x, norm2_gain[layer]), shift2, scale2)
        y = channel_mixer(h, w_ffn_up[layer], conv_w_ffn[layer], w_ffn_down[layer])
        x = x + gate2[:, None, :] * y
    return rmsnorm(x, final_gain)
```

```python
import functools

import jax
import jax.numpy as jnp
from jax import lax
from jax.experimental import pallas as pl
from jax.experimental.pallas import tpu as pltpu

F32 = jnp.float32
BF16 = jnp.bfloat16

NORM_EPS = 1e-6
LNX_EPS = 64e-5
HEAD_DIM = 64
N_ADA = 6
CONV_WIDTH = 3
LANES = 128
SUBLANES = 8
CHUNK = 64
VMEM_LIMIT = 56 * 1024 * 1024

_NN = (((1,), (0,)), ((), ()))
_NT = (((1,), (1,)), ((), ()))
_TN = (((0,), (0,)), ((), ()))


def _round_up(n, m):
    return (n + m - 1) // m * m


def _tile(n, pref, quantum):
    t = min(pref, n) // quantum * quantum
    while t > quantum and n % t:
        t -= quantum
    assert t > 0 and n % t == 0, (n, pref, quantum)
    return t


def _params(*sem):
    return pltpu.CompilerParams(dimension_semantics=sem, vmem_limit_bytes=VMEM_LIMIT)


def _ada_kernel(cb_ref, w_ref, b_ref, o_ref, *, nb, kd, tn):
    rep = tn // LANES

    def body(kc, accs):
        k0 = pl.multiple_of(kc * SUBLANES, SUBLANES)
        w = w_ref[pl.ds(k0, SUBLANES), :]
        new = []
        for b in range(nb):
            cv = cb_ref[b, pl.ds(k0, SUBLANES), :]
            cv = cv * jax.nn.sigmoid(cv)
            new.append(accs[b] + w * jnp.concatenate([cv] * rep, axis=1))
        return tuple(new)

    init = tuple(jnp.zeros((SUBLANES, tn), F32) for _ in range(nb))
    accs = lax.fori_loop(0, kd // SUBLANES, body, init, unroll=8)
    rows = [jnp.sum(a, axis=0, keepdims=True) for a in accs]
    o_ref[...] = jnp.concatenate(rows, axis=0) + b_ref[...]


def _ada(c, w, b):
    nb, kd = c.shape
    n = w.shape[1]
    tn = _tile(n, 512, LANES)
    cb = jnp.broadcast_to(c[:, :, None], (nb, kd, LANES))
    return pl.pallas_call(
        functools.partial(_ada_kernel, nb=nb, kd=kd, tn=tn),
        out_shape=jax.ShapeDtypeStruct((nb, n), F32),
        grid=(n // tn,),
        in_specs=[pl.BlockSpec((nb, kd, LANES), lambda j: (0, 0, 0)),
                  pl.BlockSpec((kd, tn), lambda j: (0, j)),
                  pl.BlockSpec((1, tn), lambda j: (0, j))],
        out_specs=pl.BlockSpec((nb, tn), lambda j: (0, j)),
        compiler_params=_params("arbitrary"),
        name="ada",
    )(cb, w, b.reshape(1, n))


def _norm_mod_kernel(x_ref, g_ref, sh_ref, sc_ref, o_ref):
    x = x_ref[...]
    y = x * lax.rsqrt(jnp.mean(x * x, axis=-1, keepdims=True) + NORM_EPS) * g_ref[...]
    o_ref[...] = (y * (1.0 + sc_ref[...]) + sh_ref[...]).astype(o_ref.dtype)


def _norm_mod(x2d, gain, mod3, shift_idx, scale_idx, rows_per_batch):
    m, d = x2d.shape
    tm = _tile(rows_per_batch, 256, SUBLANES)
    tpb = rows_per_batch // tm
    return pl.pallas_call(
        _norm_mod_kernel,
        out_shape=jax.ShapeDtypeStruct((m, d), BF16),
        grid=(m // tm,),
        in_specs=[pl.BlockSpec((tm, d), lambda i: (i, 0)),
                  pl.BlockSpec((1, d), lambda i: (0, 0)),
                  pl.BlockSpec((None, 1, d), lambda i: ((i // tpb) * N_ADA + shift_idx, 0, 0)),
                  pl.BlockSpec((None, 1, d), lambda i: ((i // tpb) * N_ADA + scale_idx, 0, 0))],
        out_specs=pl.BlockSpec((tm, d), lambda i: (i, 0)),
        compiler_params=_params("arbitrary"),
        name="norm_mod",
    )(x2d, gain.reshape(1, d), mod3, mod3)


def _shift_rows(p, prev8, s):
    rolled = pltpu.roll(p, s, 0)
    row = lax.broadcasted_iota(jnp.int32, prev8.shape, 0)
    head = jnp.where(row < s, pltpu.roll(prev8, s, 0), rolled[:SUBLANES])
    return jnp.concatenate([head, rolled[SUBLANES:]], axis=0)


def _carry_swap(carry_ref, j, tail, first):
    prev8 = jnp.where(first, 0.0, carry_ref[j])
    carry_ref[j] = tail
    return prev8


def _softplus(x):
    return jnp.maximum(x, 0.0) + jnp.log(1.0 + jnp.exp(-jnp.abs(x)))


def _proj_shift_kernel(h_ref, w_ref, mu_ref, o_ref, carry_ref, *, tpb, j_lora, n_tanh, n_lin):
    i, j = pl.program_id(0), pl.program_id(1)
    tm, tn = o_ref.shape
    p = jnp.dot(h_ref[...], w_ref[...], preferred_element_type=F32)
    prev8 = _carry_swap(carry_ref, j, p[tm - SUBLANES:], (i % tpb) == 0)
    out = p + (_shift_rows(p, prev8, 1) - p) * mu_ref[...]

    @pl.when(j < j_lora)
    def _():
        o_ref[...] = out

    @pl.when(j >= j_lora)
    def _():
        col = (j - j_lora) * tn + lax.broadcasted_iota(jnp.int32, (1, tn), 1)
        act = jnp.where(col < n_tanh, jnp.tanh(out),
                        jnp.where(col < n_tanh + n_lin, out, jax.nn.sigmoid(out)))
        o_ref[...] = act


def _proj_shift(h, w, mu, rows_per_batch, j_lora, tn, n_tanh, n_lin):
    m, d = h.shape
    n = w.shape[1]
    tm = _tile(rows_per_batch, 1024, SUBLANES)
    return pl.pallas_call(
        functools.partial(_proj_shift_kernel, tpb=rows_per_batch // tm, j_lora=j_lora,
                          n_tanh=n_tanh, n_lin=n_lin),
        out_shape=jax.ShapeDtypeStruct((m, n), F32),
        grid=(m // tm, n // tn),
        in_specs=[pl.BlockSpec((tm, d), lambda i, j: (i, 0)),
                  pl.BlockSpec((d, tn), lambda i, j: (0, j)),
                  pl.BlockSpec((1, tn), lambda i, j: (0, j))],
        out_specs=pl.BlockSpec((tm, tn), lambda i, j: (i, j)),
        scratch_shapes=[pltpu.VMEM((n // tn, SUBLANES, tn), F32)],
        compiler_params=_params("arbitrary", "arbitrary"),
        name="proj_shift",
    )(h, w, mu)


def _causal_conv3(z, prev8, cw):
    return cw[0:1] * _shift_rows(z, prev8, 2) + cw[1:2] * _shift_rows(z, prev8, 1) + cw[2:3] * z


def _proj_conv_kernel(h_ref, w_ref, cw_ref, o_ref, carry_ref, *, tpb):
    i, j = pl.program_id(0), pl.program_id(1)
    tm, tc = o_ref.shape
    p = jnp.dot(h_ref[...], w_ref[...], preferred_element_type=F32)
    z = p[:, tc:2 * tc] * p[:, 2 * tc:]
    prev8 = _carry_swap(carry_ref, j, z[tm - SUBLANES:], (i % tpb) == 0)
    o_ref[...] = (p[:, :tc] * _causal_conv3(z, prev8, cw_ref[...])).astype(o_ref.dtype)


def _proj_conv(h, w, cw, rows_per_batch, tc):
    m, d = h.shape
    dc = cw.shape[1]
    tm = _tile(rows_per_batch, 1024, SUBLANES)
    return pl.pallas_call(
        functools.partial(_proj_conv_kernel, tpb=rows_per_batch // tm),
        out_shape=jax.ShapeDtypeStruct((m, dc), BF16),
        grid=(m // tm, dc // tc),
        in_specs=[pl.BlockSpec((tm, d), lambda i, j: (i, 0)),
                  pl.BlockSpec((d, 3 * tc), lambda i, j: (0, j)),
                  pl.BlockSpec((CONV_WIDTH, tc), lambda i, j: (0, j))],
        out_specs=pl.BlockSpec((tm, tc), lambda i, j: (i, j)),
        scratch_shapes=[pltpu.VMEM((dc // tc, SUBLANES, tc), F32)],
        compiler_params=_params("arbitrary", "arbitrary"),
        name="proj_conv",
    )(h, w, cw)


def _proj_gate_kernel(h_ref, w_ref, o_ref):
    p = jnp.dot(h_ref[...], w_ref[...], preferred_element_type=F32)
    o_ref[...] = jax.nn.sigmoid(p).astype(o_ref.dtype)


def _proj_gate(h, w, rows_per_batch):
    m, d = h.shape
    n = w.shape[1]
    tm = _tile(rows_per_batch, 1024, SUBLANES)
    tn = _tile(n, 1024, LANES)
    return pl.pallas_call(
        _proj_gate_kernel,
        out_shape=jax.ShapeDtypeStruct((m, n), BF16),
        grid=(m // tm, n // tn),
        in_specs=[pl.BlockSpec((tm, d), lambda i, j: (i, 0)),
                  pl.BlockSpec((d, tn), lambda i, j: (0, j))],
        out_specs=pl.BlockSpec((tm, tn), lambda i, j: (i, j)),
        compiler_params=_params("arbitrary", "arbitrary"),
        name="proj_gate",
    )(h, w)


def _lora_kernel(a_ref, w_ref, pv_ref, o_ref, *, tiles_per_group):
    grp = pl.program_id(1) // tiles_per_group
    pre = jnp.dot(a_ref[...].astype(BF16), w_ref[...], preferred_element_type=F32) + pv_ref[...]

    @pl.when(grp == 0)
    def _():
        o_ref[...] = -jnp.exp(-_softplus(-pre) - 0.5)

    @pl.when(grp == 1)
    def _():
        o_ref[...] = jax.nn.sigmoid(pre)

    @pl.when(grp == 2)
    def _():
        o_ref[...] = pre


def _lora(ps, w, pv, rows_per_batch, lp, col_block, dr):
    m = ps.shape[0]
    n = w.shape[1]
    tm = _tile(rows_per_batch, 1024, SUBLANES)
    tn = _tile(dr, 512, LANES)
    return pl.pallas_call(
        functools.partial(_lora_kernel, tiles_per_group=dr // tn),
        out_shape=jax.ShapeDtypeStruct((m, n), F32),
        grid=(m // tm, n // tn),
        in_specs=[pl.BlockSpec((tm, lp), lambda i, j: (i, col_block)),
                  pl.BlockSpec((lp, tn), lambda i, j: (0, j)),
                  pl.BlockSpec((1, tn), lambda i, j: (0, j))],
        out_specs=pl.BlockSpec((tm, tn), lambda i, j: (i, j)),
        compiler_params=_params("arbitrary", "arbitrary"),
        name="lora",
    )(ps, w, pv)


def _split(x, n):
    pieces = []
    for _ in range(n):
        p = x.astype(BF16)
        pieces.append(p)
        x = x - p.astype(F32)
    return pieces


def _dot(a, b, dims, na=1, nb=1):
    pa = [a] if a.dtype == BF16 else _split(a, na)
    pb = [b] if b.dtype == BF16 else _split(b, nb)
    depth = max(len(pa), len(pb))
    acc = None
    for ia, xa in enumerate(pa):
        for ib, xb in enumerate(pb):
            if ia + ib >= depth:
                continue
            d = lax.dot_general(xa, xb, dims, preferred_element_type=F32)
            acc = d if acc is None else acc + d
    return acc


def _scan_kernel(r_ref, k_ref, v_ref, lw_ref, ic_ref, g_ref, kk_ref, ka_ref, rk_ref, lnw_ref, lnb_ref,
                 o_ref, st_ref, *, pairs):
    cl = CHUNK
    pw = 2 * HEAD_DIM

    @pl.when(pl.program_id(2) == 0)
    def _():
        st_ref[...] = jnp.zeros_like(st_ref)

    ri = lax.broadcasted_iota(jnp.int32, (2 * cl, 2 * cl), 0)
    ci = lax.broadcasted_iota(jnp.int32, (2 * cl, 2 * cl), 1)
    strict = ri > ci
    incl = ri >= ci
    eye = jnp.where(ri == ci, 1.0, 0.0).astype(F32)
    hi = lax.broadcasted_iota(jnp.int32, (pw, pw), 0) // HEAD_DIM
    hj = lax.broadcasted_iota(jnp.int32, (pw, pw), 1) // HEAD_DIM
    head_ones = jnp.where(hi == hj, 1.0, 0.0).astype(BF16)
    ti = lax.broadcasted_iota(jnp.int32, (cl, cl), 0)
    tj = lax.broadcasted_iota(jnp.int32, (cl, cl), 1)
    cum_ones = jnp.where(ti >= tj, 1.0, 0.0).astype(BF16)
    all_ones = jnp.ones((cl, pw), BF16)
    first_head = lax.broadcasted_iota(jnp.int32, (1, pw), 1) < HEAD_DIM
    inv_n = 1.0 / HEAD_DIM

    def block_diag(x):
        return jnp.concatenate([jnp.where(first_head, x, 0.0), jnp.where(first_head, 0.0, x)], axis=0)

    def head_sum(x):
        return _dot(x, head_ones, _NN, na=2)

    for pp in range(pairs):
        sl = slice(pp * pw, (pp + 1) * pw)
        r, kraw, v = r_ref[:, sl], k_ref[:, sl], v_ref[:, sl]
        lw, ic, g = lw_ref[:, sl], ic_ref[:, sl], g_ref[:, sl]

        kk = kraw * kk_ref[:, sl]
        kk = kk / jnp.maximum(jnp.sqrt(head_sum(kk * kk)), 1e-12)
        kmod = kraw * (1.0 + (ic - 1.0) * ka_ref[:, sl])
        a = -kk
        b = kk * ic

        lg = _dot(cum_ones, lw, _NN, nb=3)
        ltot = lg[cl - 1:cl, :]
        g_inv = jnp.exp(-lg)
        g_end = jnp.exp(ltot - lg)
        a_bd = block_diag(a * jnp.exp(lg - lw))
        r_bd = block_diag(r * jnp.exp(lg))
        b_bd = block_diag(b * g_inv)
        k_bd = block_diag(kmod * g_inv)
        v_bd = block_diag(v)
        bend_bd = block_diag(b * g_end)
        kend_bd = block_diag(kmod * g_end)
        gcol = jnp.exp(_dot(lw, all_ones, _TN, na=3))

        sc = _dot(jnp.concatenate([a_bd, r_bd], axis=0), jnp.concatenate([b_bd, k_bd], axis=0), _NT)
        l_ab = jnp.where(strict, sc[:2 * cl, :2 * cl], 0.0)
        m_ak = jnp.where(strict, sc[:2 * cl, 2 * cl:], 0.0)
        m_rb = jnp.where(incl, sc[2 * cl:, :2 * cl], 0.0)
        m_rk = jnp.where(incl, sc[2 * cl:, 2 * cl:], 0.0)

        tinv = eye + l_ab
        pw_l = l_ab
        for _ in range(cl.bit_length() - 2):
            pw_l = _dot(pw_l, pw_l, _NN)
            tinv = tinv + _dot(tinv, pw_l, _NN)

        st = st_ref[pp]
        u = _dot(tinv, _dot(a_bd, st, _NN) + _dot(m_ak, v_bd, _NN), _NN)
        y_bd = _dot(r_bd, st, _NN) + _dot(m_rb, u, _NN) + _dot(m_rk, v_bd, _NN)
        st_ref[pp] = gcol * st + _dot(bend_bd, u, _TN) + _dot(kend_bd, v_bd, _TN)
        y = y_bd[:cl] + y_bd[cl:]

        mean = head_sum(y) * inv_n
        dy = y - mean
        var = head_sum(dy * dy) * inv_n
        yn = dy * lax.rsqrt(var + LNX_EPS) * lnw_ref[:, sl] + lnb_ref[:, sl]
        bonus = head_sum(r * kmod * rk_ref[:, sl]) * v
        o_ref[:, sl] = ((yn + bonus) * g).astype(o_ref.dtype)


def _scan(ps, lig, k_k, k_a, r_k, lnx_w, lnx_b, nb, seq, dr):
    m = ps.shape[0]
    pw = 2 * HEAD_DIM
    pairs = 4 if dr % (4 * pw) == 0 else 1
    bw = pairs * pw
    npb = dr // bw
    nc = seq // CHUNK
    tok = lambda off: pl.BlockSpec((CHUNK, bw), lambda b, p, c: (b * nc + c, off * npb + p))
    par = pl.BlockSpec((1, bw), lambda b, p, c: (0, p))
    return pl.pallas_call(
        functools.partial(_scan_kernel, pairs=pairs),
        out_shape=jax.ShapeDtypeStruct((m, dr), BF16),
        grid=(nb, npb, nc),
        in_specs=[tok(0), tok(1), tok(2), tok(0), tok(1), tok(2), par, par, par, par, par],
        out_specs=pl.BlockSpec((CHUNK, bw), lambda b, p, c: (b * nc + c, p)),
        scratch_shapes=[pltpu.VMEM((pairs, pw, pw), F32)],
        compiler_params=_params("arbitrary", "arbitrary", "arbitrary"),
        name="rwkv_scan",
    )(ps, ps, ps, lig, lig, lig, k_k, k_a, r_k, lnx_w, lnx_b)


def _merge_kernel(o_ref, y_ref, wa_ref, wb_ref, sa_ref, sb_ref, out_ref):
    ya = jnp.dot(o_ref[...], wa_ref[...], preferred_element_type=F32)
    yb = jnp.dot(y_ref[...], wb_ref[...], preferred_element_type=F32)
    out_ref[...] = (sa_ref[...].astype(F32) * ya + sb_ref[...].astype(F32) * yb).astype(out_ref.dtype)


def _merge(o, ycb, wa, wb, sg, rows_per_batch):
    m, dr = o.shape
    dc = ycb.shape[1]
    d = wa.shape[1]
    tm = _tile(rows_per_batch, 1024, SUBLANES)
    tn = _tile(d, 512, LANES)
    nj = d // tn
    return pl.pallas_call(
        _merge_kernel,
        out_shape=jax.ShapeDtypeStruct((m, d), BF16),
        grid=(m // tm, nj),
        in_specs=[pl.BlockSpec((tm, dr), lambda i, j: (i, 0)),
                  pl.BlockSpec((tm, dc), lambda i, j: (i, 0)),
                  pl.BlockSpec((dr, tn), lambda i, j: (0, j)),
                  pl.BlockSpec((dc, tn), lambda i, j: (0, j)),
                  pl.BlockSpec((tm, tn), lambda i, j: (i, j)),
                  pl.BlockSpec((tm, tn), lambda i, j: (i, nj + j))],
        out_specs=pl.BlockSpec((tm, tn), lambda i, j: (i, j)),
        compiler_params=_params("arbitrary", "arbitrary"),
        name="merge",
    )(o, ycb, wa, wb, sg, sg)


def _resid_kernel(a_ref, w_ref, x_ref, gt_ref, o_ref):
    y = jnp.dot(a_ref[...], w_ref[...], preferred_element_type=F32)
    o_ref[...] = x_ref[...] + gt_ref[...] * y


def _resid(a, w, x2d, mod3, gate_idx, rows_per_batch):
    m, kd = a.shape
    d = w.shape[1]
    tm = _tile(rows_per_batch, 1024, SUBLANES)
    tn = _tile(d, 512, LANES)
    tpb = rows_per_batch // tm
    return pl.pallas_call(
        _resid_kernel,
        out_shape=jax.ShapeDtypeStruct((m, d), F32),
        grid=(m // tm, d // tn),
        in_specs=[pl.BlockSpec((tm, kd), lambda i, j: (i, 0)),
                  pl.BlockSpec((kd, tn), lambda i, j: (0, j)),
                  pl.BlockSpec((tm, tn), lambda i, j: (i, j)),
                  pl.BlockSpec((None, 1, tn), lambda i, j: ((i // tpb) * N_ADA + gate_idx, 0, j))],
        out_specs=pl.BlockSpec((tm, tn), lambda i, j: (i, j)),
        compiler_params=_params("arbitrary", "arbitrary"),
        name="resid",
    )(a, w, x2d, mod3)


def _ffn_up_kernel(h_ref, w_ref, cw_ref, o_ref, carry_ref, *, tpb):
    i, j = pl.program_id(0), pl.program_id(1)
    tm, tc = o_ref.shape
    p = jnp.dot(h_ref[...], w_ref[...], preferred_element_type=F32)
    gate = p[:, :tc]
    prev8 = _carry_swap(carry_ref, j, gate[tm - SUBLANES:], (i % tpb) == 0)
    gc = _causal_conv3(gate, prev8, cw_ref[...])
    o_ref[...] = (gc * jax.nn.sigmoid(gc) * p[:, tc:]).astype(o_ref.dtype)


def _ffn_up(h, w, cw, rows_per_batch, tc):
    m, d = h.shape
    fp = cw.shape[1]
    tm = _tile(rows_per_batch, 1024, SUBLANES)
    return pl.pallas_call(
        functools.partial(_ffn_up_kernel, tpb=rows_per_batch // tm),
        out_shape=jax.ShapeDtypeStruct((m, fp), BF16),
        grid=(m // tm, fp // tc),
        in_specs=[pl.BlockSpec((tm, d), lambda i, j: (i, 0)),
                  pl.BlockSpec((d, 2 * tc), lambda i, j: (0, j)),
                  pl.BlockSpec((CONV_WIDTH, tc), lambda i, j: (0, j))],
        out_specs=pl.BlockSpec((tm, tc), lambda i, j: (i, j)),
        scratch_shapes=[pltpu.VMEM((fp // tc, SUBLANES, tc), F32)],
        compiler_params=_params("arbitrary", "arbitrary"),
        name="ffn_up",
    )(h, w, cw)


def _ffn_down_kernel(a_ref, w_ref, x_ref, gt_ref, fg_ref, o_ref, *, final_norm):
    k = pl.program_id(1)
    y = jnp.dot(a_ref[...], w_ref[...], preferred_element_type=F32)

    @pl.when(k == 0)
    def _():
        o_ref[...] = y

    @pl.when(k > 0)
    def _():
        o_ref[...] += y

    @pl.when(k == pl.num_programs(1) - 1)
    def _():
        rows = _tile(o_ref.shape[0], 64, SUBLANES)

        def finish(rc, carry):
            rs = pl.ds(pl.multiple_of(rc * rows, rows), rows)
            x2 = x_ref[rs, :] + gt_ref[...] * o_ref[rs, :]
            if final_norm:
                x2 = x2 * lax.rsqrt(jnp.mean(x2 * x2, axis=-1, keepdims=True) + NORM_EPS) * fg_ref[...]
            o_ref[rs, :] = x2
            return carry

        lax.fori_loop(0, o_ref.shape[0] // rows, finish, 0)


def _ffn_down(act, w, x2d, mod3, gate_idx, final_gain, final_norm, rows_per_batch):
    m, fp = act.shape
    d = w.shape[1]
    tm = _tile(rows_per_batch, 512, SUBLANES)
    tk = _tile(fp, 512, LANES)
    tpb = rows_per_batch // tm
    return pl.pallas_call(
        functools.partial(_ffn_down_kernel, final_norm=final_norm),
        out_shape=jax.ShapeDtypeStruct((m, d), F32),
        grid=(m // tm, fp // tk),
        in_specs=[pl.BlockSpec((tm, tk), lambda i, k: (i, k)),
                  pl.BlockSpec((tk, d), lambda i, k: (k, 0)),
                  pl.BlockSpec((tm, d), lambda i, k: (i, 0)),
                  pl.BlockSpec((None, 1, d), lambda i, k: ((i // tpb) * N_ADA + gate_idx, 0, 0)),
                  pl.BlockSpec((1, d), lambda i, k: (0, 0))],
        out_specs=pl.BlockSpec((tm, d), lambda i, k: (i, 0)),
        compiler_params=_params("arbitrary", "arbitrary"),
        name="ffn_down",
    )(act, w, x2d, mod3, final_gain.reshape(1, d))


def _interleave_cols(w, groups, tc):
    kd, total = w.shape
    n = total // groups
    return w.reshape(kd, groups, n // tc, tc).transpose(0, 2, 1, 3).reshape(kd, total)


def _pad_cols(w, n):
    return jnp.pad(w, ((0, 0), (0, n - w.shape[1])))


def kernel(x, c, w_ada, b_ada, norm1_gain, w_in, mu_shift, w0, a0, k_k, k_a, r_k, w_lora_decay,
           w_lora_iclr, w_lora_gate, lnx_w, lnx_b, conv_w_mix, w_o_rwkv, w_o_conv, w_out, norm2_gain,
           w_ffn_up, conv_w_ffn, w_ffn_down, final_gain):
    nb, seq, d = x.shape
    m = nb * seq
    depth = w_ada.shape[0]
    dr = w_o_rwkv.shape[1]
    dc = w_o_conv.shape[1]
    dff = w_ffn_down.shape[1]
    n_dec, n_icl, n_gat = w_lora_decay.shape[1], w_lora_iclr.shape[1], w_lora_gate.shape[1]
    n_lora = n_dec + n_icl + n_gat
    n_shift = 3 * dr + n_lora
    assert seq % CHUNK == 0 and dr % (2 * HEAD_DIM) == 0

    tn_a = _tile(dr, 512, LANES)
    lp = _round_up(n_lora, tn_a)
    assert (3 * dr) % lp == 0
    tc_conv = _tile(dc, 256, LANES)
    tc_ffn = 256
    fp = _round_up(dff, 1024)

    x2d = x.reshape(m, d)
    c_act_in = c
    for layer in range(depth):
        wi = w_in[layer]
        w_a = jnp.concatenate([wi[:, :3 * dr], _pad_cols(wi[:, 3 * dr:n_shift], lp)], axis=1).astype(BF16)
        mu_a = jnp.concatenate([mu_shift[layer][:3 * dr], jnp.pad(mu_shift[layer][3 * dr:], (0, lp - n_lora))])
        w_c = _interleave_cols(wi[:, n_shift:n_shift + 3 * dc], 3, tc_conv).astype(BF16)
        w_g = wi[:, n_shift + 3 * dc:].astype(BF16)
        w_l = jnp.zeros((lp, 3 * dr), F32)
        w_l = w_l.at[:n_dec, :dr].set(w_lora_decay[layer])
        w_l = w_l.at[n_dec:n_dec + n_icl, dr:2 * dr].set(w_lora_iclr[layer])
        w_l = w_l.at[n_dec + n_icl:n_lora, 2 * dr:].set(w_lora_gate[layer]).astype(BF16)
        pv_l = jnp.concatenate([w0[layer], a0[layer], jnp.zeros((dr,), F32)]).reshape(1, 3 * dr)
        wu = w_ffn_up[layer]
        w_u = _interleave_cols(
            jnp.concatenate([_pad_cols(wu[:, :dff], fp), _pad_cols(wu[:, dff:], fp)], axis=1), 2, tc_ffn
        ).astype(BF16)
        cw_f = _pad_cols(conv_w_ffn[layer], fp)
        w_d = jnp.pad(w_ffn_down[layer], ((0, fp - dff), (0, 0))).astype(BF16)
        row = lambda p: p.reshape(1, dr)

        mod3 = _ada(c_act_in, w_ada[layer], b_ada[layer]).reshape(nb * N_ADA, 1, d)
        h = _norm_mod(x2d, norm1_gain[layer], mod3, 0, 1, seq)
        ps = _proj_shift(h, w_a, mu_a.reshape(1, -1), seq, 3 * dr // tn_a, tn_a, n_dec, n_icl)
        ycb = _proj_conv(h, w_c, conv_w_mix[layer], seq, tc_conv)
        sg = _proj_gate(h, w_g, seq)
        lig = _lora(ps, w_l, pv_l, seq, lp, 3 * dr // lp, dr)
        o = _scan(ps, lig, row(k_k[layer]), row(k_a[layer]), row(r_k[layer]), row(lnx_w[layer]),
                  row(lnx_b[layer]), nb, seq, dr)
        merged = _merge(o, ycb, w_o_rwkv[layer].astype(BF16), w_o_conv[layer].astype(BF16), sg, seq)
        x1 = _resid(merged, w_out[layer].astype(BF16), x2d, mod3, 2, seq)
        h2 = _norm_mod(x1, norm2_gain[layer], mod3, 3, 4, seq)
        act = _ffn_up(h2, w_u, cw_f, seq, tc_ffn)
        x2d = _ffn_down(act, w_d, x1, mod3, 5, final_gain, layer == depth - 1, seq)
    if depth == 0:
        raise ValueError("at least one layer is required")
    return x2d.reshape(nb, seq, d)
```

```python
import functools

import jax
import jax.numpy as jnp
from jax import lax
from jax.experimental import pallas as pl
from jax.experimental.pallas import tpu as pltpu

F32 = jnp.float32
BF16 = jnp.bfloat16

NORM_EPS = 1e-6
LNX_EPS = 64e-5
HEAD_DIM = 64
N_ADA = 6
CONV_WIDTH = 3
LANES = 128
SUBLANES = 8
CHUNK = 64
VMEM_LIMIT = 56 * 1024 * 1024

_NN = (((1,), (0,)), ((), ()))
_NT = (((1,), (1,)), ((), ()))
_TN = (((0,), (0,)), ((), ()))


def _round_up(n, m):
    return (n + m - 1) // m * m


def _tile(n, pref, quantum):
    t = min(pref, n) // quantum * quantum
    while t > quantum and n % t:
        t -= quantum
    assert t > 0 and n % t == 0, (n, pref, quantum)
    return t


def _params(*sem):
    return pltpu.CompilerParams(dimension_semantics=sem, vmem_limit_bytes=VMEM_LIMIT)


def _ada_kernel(cb_ref, w_ref, b_ref, o_ref, *, nb, kd, tn):
    rep = tn // LANES

    def body(kc, accs):
        k0 = pl.multiple_of(kc * SUBLANES, SUBLANES)
        w = w_ref[pl.ds(k0, SUBLANES), :]
        new = []
        for b in range(nb):
            cv = cb_ref[b, pl.ds(k0, SUBLANES), :]
            cv = cv * jax.nn.sigmoid(cv)
            new.append(accs[b] + w * jnp.concatenate([cv] * rep, axis=1))
        return tuple(new)

    init = tuple(jnp.zeros((SUBLANES, tn), F32) for _ in range(nb))
    accs = lax.fori_loop(0, kd // SUBLANES, body, init, unroll=8)
    rows = [jnp.sum(a, axis=0, keepdims=True) for a in accs]
    o_ref[...] = jnp.concatenate(rows, axis=0) + b_ref[...]


def _ada(c, w, b):
    nb, kd = c.shape
    n = w.shape[1]
    tn = _tile(n, 512, LANES)
    cb = jnp.broadcast_to(c[:, :, None], (nb, kd, LANES))
    return pl.pallas_call(
        functools.partial(_ada_kernel, nb=nb, kd=kd, tn=tn),
        out_shape=jax.ShapeDtypeStruct((nb, n), F32),
        grid=(n // tn,),
        in_specs=[pl.BlockSpec((nb, kd, LANES), lambda j: (0, 0, 0)),
                  pl.BlockSpec((kd, tn), lambda j: (0, j)),
                  pl.BlockSpec((1, tn), lambda j: (0, j))],
        out_specs=pl.BlockSpec((nb, tn), lambda j: (0, j)),
        compiler_params=_params("arbitrary"),
        name="ada",
    )(cb, w, b.reshape(1, n))


def _rms(x):
    return x * lax.rsqrt(jnp.mean(x * x, axis=-1, keepdims=True) + NORM_EPS)


def _norm_mod_kernel(x_ref, g_ref, sh_ref, sc_ref, o_ref):
    y = _rms(x_ref[...]) * g_ref[...]
    o_ref[...] = (y * (1.0 + sc_ref[...]) + sh_ref[...]).astype(o_ref.dtype)


def _norm_mod(x2d, gain, mod3, shift_idx, scale_idx, rows_per_batch):
    m, d = x2d.shape
    tm = _tile(rows_per_batch, 256, SUBLANES)
    tpb = rows_per_batch // tm
    return pl.pallas_call(
        _norm_mod_kernel,
        out_shape=jax.ShapeDtypeStruct((m, d), BF16),
        grid=(m // tm,),
        in_specs=[pl.BlockSpec((tm, d), lambda i: (i, 0)),
                  pl.BlockSpec((1, d), lambda i: (0, 0)),
                  pl.BlockSpec((None, 1, d), lambda i: ((i // tpb) * N_ADA + shift_idx, 0, 0)),
                  pl.BlockSpec((None, 1, d), lambda i: ((i // tpb) * N_ADA + scale_idx, 0, 0))],
        out_specs=pl.BlockSpec((tm, d), lambda i: (i, 0)),
        compiler_params=_params("arbitrary"),
        name="norm_mod",
    )(x2d, gain.reshape(1, d), mod3, mod3)


def _final_norm_kernel(x_ref, g_ref, o_ref):
    o_ref[...] = _rms(x_ref[...]) * g_ref[...]


def _final_norm(x2d, gain):
    m, d = x2d.shape
    tm = _tile(m, 256, SUBLANES)
    return pl.pallas_call(
        _final_norm_kernel,
        out_shape=jax.ShapeDtypeStruct((m, d), F32),
        grid=(m // tm,),
        in_specs=[pl.BlockSpec((tm, d), lambda i: (i, 0)),
                  pl.BlockSpec((1, d), lambda i: (0, 0))],
        out_specs=pl.BlockSpec((tm, d), lambda i: (i, 0)),
        compiler_params=_params("arbitrary"),
        name="final_norm",
    )(x2d, gain.reshape(1, d))


def _shift_rows(p, prev8, s):
    rolled = pltpu.roll(p, s, 0)
    row = lax.broadcasted_iota(jnp.int32, prev8.shape, 0)
    head = jnp.where(row < s, pltpu.roll(prev8, s, 0), rolled[:SUBLANES])
    return jnp.concatenate([head, rolled[SUBLANES:]], axis=0)


def _carry_swap(carry_ref, j, tail, first):
    prev8 = jnp.where(first, 0.0, carry_ref[j])
    carry_ref[j] = tail
    return prev8


def _causal_conv3(z, prev8, cw):
    return cw[0:1] * _shift_rows(z, prev8, 2) + cw[1:2] * _shift_rows(z, prev8, 1) + cw[2:3] * z


def _softplus(x):
    return jnp.maximum(x, 0.0) + jnp.log(1.0 + jnp.exp(-jnp.abs(x)))


def _proj_shift_kernel(h_ref, w_ref, mu_ref, o_ref, carry_ref, *, tpb, j_lora, n_tanh, n_lin):
    i, j = pl.program_id(0), pl.program_id(1)
    tm, tn = o_ref.shape
    p = jnp.dot(h_ref[...], w_ref[...], preferred_element_type=F32)
    prev8 = _carry_swap(carry_ref, j, p[tm - SUBLANES:], (i % tpb) == 0)
    out = p + (_shift_rows(p, prev8, 1) - p) * mu_ref[...]

    @pl.when(j < j_lora)
    def _():
        o_ref[...] = out

    @pl.when(j >= j_lora)
    def _():
        col = (j - j_lora) * tn + lax.broadcasted_iota(jnp.int32, (1, tn), 1)
        act = jnp.where(col < n_tanh, jnp.tanh(out),
                        jnp.where(col < n_tanh + n_lin, out, jax.nn.sigmoid(out)))
        o_ref[...] = act


def _proj_shift(h, w, mu, rows_per_batch, j_lora, tn, n_tanh, n_lin):
    m, d = h.shape
    n = w.shape[1]
    tm = _tile(rows_per_batch, 1024, SUBLANES)
    return pl.pallas_call(
        functools.partial(_proj_shift_kernel, tpb=rows_per_batch // tm, j_lora=j_lora,
                          n_tanh=n_tanh, n_lin=n_lin),
        out_shape=jax.ShapeDtypeStruct((m, n), F32),
        grid=(m // tm, n // tn),
        in_specs=[pl.BlockSpec((tm, d), lambda i, j: (i, 0)),
                  pl.BlockSpec((d, tn), lambda i, j: (0, j)),
                  pl.BlockSpec((1, tn), lambda i, j: (0, j))],
        out_specs=pl.BlockSpec((tm, tn), lambda i, j: (i, j)),
        scratch_shapes=[pltpu.VMEM((n // tn, SUBLANES, tn), F32)],
        compiler_params=_params("arbitrary", "arbitrary"),
        name="proj_shift",
    )(h, w, mu)


def _proj_conv_kernel(h_ref, wb_ref, wc_ref, wx_ref, cw_ref, o_ref, carry_ref, *, tpb):
    i, j = pl.program_id(0), pl.program_id(1)
    tm = o_ref.shape[0]
    h = h_ref[...]
    z = (jnp.dot(h, wc_ref[...], preferred_element_type=F32)
         * jnp.dot(h, wx_ref[...], preferred_element_type=F32))
    prev8 = _carry_swap(carry_ref, j, z[tm - SUBLANES:], (i % tpb) == 0)
    conv = _causal_conv3(z, prev8, cw_ref[...])
    o_ref[...] = (jnp.dot(h, wb_ref[...], preferred_element_type=F32) * conv).astype(o_ref.dtype)


def _proj_conv(h, w, cw, rows_per_batch):
    m, d = h.shape
    dc = cw.shape[1]
    tm = _tile(rows_per_batch, 1024, SUBLANES)
    tc = _tile(dc, 512, LANES)
    ncb = dc // tc
    wspec = lambda grp: pl.BlockSpec((d, tc), lambda i, j: (0, grp * ncb + j))
    return pl.pallas_call(
        functools.partial(_proj_conv_kernel, tpb=rows_per_batch // tm),
        out_shape=jax.ShapeDtypeStruct((m, dc), BF16),
        grid=(m // tm, ncb),
        in_specs=[pl.BlockSpec((tm, d), lambda i, j: (i, 0)), wspec(0), wspec(1), wspec(2),
                  pl.BlockSpec((CONV_WIDTH, tc), lambda i, j: (0, j))],
        out_specs=pl.BlockSpec((tm, tc), lambda i, j: (i, j)),
        scratch_shapes=[pltpu.VMEM((ncb, SUBLANES, tc), F32)],
        compiler_params=_params("arbitrary", "arbitrary"),
        name="proj_conv",
    )(h, w, w, w, cw)


def _proj_gate_kernel(h_ref, w_ref, o_ref):
    p = jnp.dot(h_ref[...], w_ref[...], preferred_element_type=F32)
    o_ref[...] = jax.nn.sigmoid(p).astype(o_ref.dtype)


def _proj_gate(h, w, rows_per_batch):
    m, d = h.shape
    n = w.shape[1]
    tm = _tile(rows_per_batch, 1024, SUBLANES)
    tn = _tile(n, 1024, LANES)
    return pl.pallas_call(
        _proj_gate_kernel,
        out_shape=jax.ShapeDtypeStruct((m, n), BF16),
        grid=(m // tm, n // tn),
        in_specs=[pl.BlockSpec((tm, d), lambda i, j: (i, 0)),
                  pl.BlockSpec((d, tn), lambda i, j: (0, j))],
        out_specs=pl.BlockSpec((tm, tn), lambda i, j: (i, j)),
        compiler_params=_params("arbitrary", "arbitrary"),
        name="proj_gate",
    )(h, w)


def _lora_kernel(a_ref, w_ref, pv_ref, o_ref, *, tiles_per_group):
    grp = pl.program_id(1) // tiles_per_group
    pre = jnp.dot(a_ref[...].astype(BF16), w_ref[...], preferred_element_type=F32) + pv_ref[...]

    @pl.when(grp == 0)
    def _():
        o_ref[...] = -jnp.exp(-_softplus(-pre) - 0.5)

    @pl.when(grp == 1)
    def _():
        o_ref[...] = jax.nn.sigmoid(pre)

    @pl.when(grp == 2)
    def _():
        o_ref[...] = pre


def _lora(ps, w, pv, rows_per_batch, lp, col_block, dr):
    m = ps.shape[0]
    n = w.shape[1]
    tm = _tile(rows_per_batch, 1024, SUBLANES)
    tn = _tile(dr, 512, LANES)
    return pl.pallas_call(
        functools.partial(_lora_kernel, tiles_per_group=dr // tn),
        out_shape=jax.ShapeDtypeStruct((m, n), F32),
        grid=(m // tm, n // tn),
        in_specs=[pl.BlockSpec((tm, lp), lambda i, j: (i, col_block)),
                  pl.BlockSpec((lp, tn), lambda i, j: (0, j)),
                  pl.BlockSpec((1, tn), lambda i, j: (0, j))],
        out_specs=pl.BlockSpec((tm, tn), lambda i, j: (i, j)),
        compiler_params=_params("arbitrary", "arbitrary"),
        name="lora",
    )(ps, w, pv)


def _split(x, n):
    pieces = []
    for _ in range(n):
        p = x.astype(BF16)
        pieces.append(p)
        x = x - p.astype(F32)
    return pieces


def _dot(a, b, dims, na=1, nb=1):
    pa = [a] if a.dtype == BF16 else _split(a, na)
    pb = [b] if b.dtype == BF16 else _split(b, nb)
    depth = max(len(pa), len(pb))
    acc = None
    for ia, xa in enumerate(pa):
        for ib, xb in enumerate(pb):
            if ia + ib >= depth:
                continue
            d = lax.dot_general(xa, xb, dims, preferred_element_type=F32)
            acc = d if acc is None else acc + d
    return acc


def _scan_kernel(r_ref, k_ref, v_ref, lw_ref, ic_ref, g_ref, kk_ref, ka_ref, rk_ref, lnw_ref, lnb_ref,
                 o_ref, st_ref, *, nb, pairs):
    cl = CHUNK
    pw = 2 * HEAD_DIM
    chains = [(b, pp) for b in range(nb) for pp in range(pairs)]
    nch = len(chains)

    @pl.when(pl.program_id(1) == 0)
    def _():
        st_ref[...] = jnp.zeros_like(st_ref)

    ri = lax.broadcasted_iota(jnp.int32, (2 * cl, 2 * cl), 0)
    ci = lax.broadcasted_iota(jnp.int32, (2 * cl, 2 * cl), 1)
    strict = ri > ci
    incl = ri >= ci
    eye = jnp.where(ri == ci, 1.0, 0.0).astype(F32)
    hi = lax.broadcasted_iota(jnp.int32, (pw, pw), 0) // HEAD_DIM
    hj = lax.broadcasted_iota(jnp.int32, (pw, pw), 1) // HEAD_DIM
    head_ones = jnp.where(hi == hj, 1.0, 0.0).astype(BF16)
    ti = lax.broadcasted_iota(jnp.int32, (cl, cl), 0)
    tj = lax.broadcasted_iota(jnp.int32, (cl, cl), 1)
    cum_ones = jnp.where(ti >= tj, 1.0, 0.0).astype(BF16)
    all_ones = jnp.ones((cl, pw), BF16)
    first_head = lax.broadcasted_iota(jnp.int32, (1, pw), 1) < HEAD_DIM
    inv_n = 1.0 / HEAD_DIM

    def block_diag(x):
        return jnp.concatenate([jnp.where(first_head, x, 0.0), jnp.where(first_head, 0.0, x)], axis=0)

    def head_sums(xs):
        s = _dot(jnp.concatenate(xs, axis=0), head_ones, _NN, na=2)
        return [s[i * cl:(i + 1) * cl] for i in range(nch)]

    def tok(ref):
        return [ref[b, :, pp * pw:(pp + 1) * pw] for b, pp in chains]

    def par(ref):
        return [ref[:, pp * pw:(pp + 1) * pw] for _, pp in chains]

    def each(f, *ls):
        return [f(*xs) for xs in zip(*ls)]

    r, kraw, v, lw, ic, g = tok(r_ref), tok(k_ref), tok(v_ref), tok(lw_ref), tok(ic_ref), tok(g_ref)

    kk = each(lambda k, p: k * p, kraw, par(kk_ref))
    ss = head_sums(each(lambda x: x * x, kk))
    kk = each(lambda x, s: x / jnp.maximum(jnp.sqrt(s), 1e-12), kk, ss)
    kmod = each(lambda k, i, p: k * (1.0 + (i - 1.0) * p), kraw, ic, par(ka_ref))
    bvec = each(lambda x, i: x * i, kk, ic)

    lg_b = [_dot(cum_ones, lw_ref[b], _NN, nb=3) for b in range(nb)]
    gc_b = [_dot(lw_ref[b], all_ones, _TN, na=3) for b in range(nb)]
    lg = [lg_b[b][:, pp * pw:(pp + 1) * pw] for b, pp in chains]
    gcol = [jnp.exp(gc_b[b][pp * pw:(pp + 1) * pw]) for b, pp in chains]
    g_inv = each(lambda x: jnp.exp(-x), lg)
    g_end = each(lambda x: jnp.exp(x[cl - 1:cl, :] - x), lg)

    a_bd = each(lambda x, l, w: block_diag(-x * jnp.exp(l - w)), kk, lg, lw)
    r_bd = each(lambda x, l: block_diag(x * jnp.exp(l)), r, lg)
    b_bd = each(lambda x, gi: block_diag(x * gi), bvec, g_inv)
    k_bd = each(lambda x, gi: block_diag(x * gi), kmod, g_inv)
    v_bd = each(block_diag, v)
    bend_bd = each(lambda x, ge: block_diag(x * ge), bvec, g_end)
    kend_bd = each(lambda x, ge: block_diag(x * ge), kmod, g_end)

    sc = each(lambda a, rr, b, k: _dot(jnp.concatenate([a, rr], axis=0), jnp.concatenate([b, k], axis=0), _NT),
              a_bd, r_bd, b_bd, k_bd)
    l_ab = each(lambda s: jnp.where(strict, s[:2 * cl, :2 * cl], 0.0), sc)
    m_ak = each(lambda s: jnp.where(strict, s[:2 * cl, 2 * cl:], 0.0), sc)
    m_rb = each(lambda s: jnp.where(incl, s[2 * cl:, :2 * cl], 0.0), sc)
    m_rk = each(lambda s: jnp.where(incl, s[2 * cl:, 2 * cl:], 0.0), sc)

    tinv = each(lambda l: eye + l, l_ab)
    lpow = l_ab
    for _ in range(cl.bit_length() - 2):
        lpow = each(lambda p: _dot(p, p, _NN), lpow)
        tinv = each(lambda t, p: t + _dot(t, p, _NN), tinv, lpow)

    st = [st_ref[i] for i in range(nch)]
    mv = each(lambda mm, vv: _dot(mm, vv, _NN), m_ak, v_bd)
    rhs = each(lambda a, s, x: _dot(a, s, _NN) + x, a_bd, st, mv)
    u = each(lambda t, x: _dot(t, x, _NN), tinv, rhs)
    y0 = each(lambda rr, s, mm, vv: _dot(rr, s, _NN) + _dot(mm, vv, _NN), r_bd, st, m_rk, v_bd)
    y_bd = each(lambda mm, uu, y: _dot(mm, uu, _NN) + y, m_rb, u, y0)
    for i in range(nch):
        st_ref[i] = gcol[i] * st[i] + _dot(bend_bd[i], u[i], _TN) + _dot(kend_bd[i], v_bd[i], _TN)
    y = each(lambda x: x[:cl] + x[cl:], y_bd)

    mean = each(lambda s: s * inv_n, head_sums(y))
    dy = each(lambda a, b: a - b, y, mean)
    var = each(lambda s: s * inv_n, head_sums(each(lambda x: x * x, dy)))
    bon = head_sums(each(lambda a, b, p: a * b * p, r, kmod, par(rk_ref)))
    for i, (b, pp) in enumerate(chains):
        sl = slice(pp * pw, (pp + 1) * pw)
        yn = dy[i] * lax.rsqrt(var[i] + LNX_EPS) * lnw_ref[:, sl] + lnb_ref[:, sl]
        o_ref[b, :, sl] = ((yn + bon[i] * v[i]) * g[i]).astype(o_ref.dtype)


def _scan(ps, lig, k_k, k_a, r_k, lnx_w, lnx_b, nb, seq, dr):
    pw = 2 * HEAD_DIM
    pairs = 4 if dr % (4 * pw) == 0 else 1
    bw = pairs * pw
    npb = dr // bw
    nc = seq // CHUNK
    ps3 = ps.reshape(nb, seq, ps.shape[1])
    lig3 = lig.reshape(nb, seq, lig.shape[1])
    tok = lambda off: pl.BlockSpec((nb, CHUNK, bw), lambda p, c: (0, c, off * npb + p))
    par = pl.BlockSpec((1, bw), lambda p, c: (0, p))
    o = pl.pallas_call(
        functools.partial(_scan_kernel, nb=nb, pairs=pairs),
        out_shape=jax.ShapeDtypeStruct((nb, seq, dr), BF16),
        grid=(npb, nc),
        in_specs=[tok(0), tok(1), tok(2), tok(0), tok(1), tok(2), par, par, par, par, par],
        out_specs=pl.BlockSpec((nb, CHUNK, bw), lambda p, c: (0, c, p)),
        scratch_shapes=[pltpu.VMEM((nb * pairs, pw, pw), F32)],
        compiler_params=_params("arbitrary", "arbitrary"),
        name="rwkv_scan",
    )(ps3, ps3, ps3, lig3, lig3, lig3, k_k, k_a, r_k, lnx_w, lnx_b)
    return o.reshape(nb * seq, dr)


def _merge_kernel(o_ref, y_ref, wa_ref, wb_ref, sa_ref, sb_ref, out_ref):
    ya = jnp.dot(o_ref[...], wa_ref[...], preferred_element_type=F32)
    yb = jnp.dot(y_ref[...], wb_ref[...], preferred_element_type=F32)
    out_ref[...] = (sa_ref[...].astype(F32) * ya + sb_ref[...].astype(F32) * yb).astype(out_ref.dtype)


def _merge(o, ycb, wa, wb, sg, rows_per_batch):
    m, dr = o.shape
    dc = ycb.shape[1]
    d = wa.shape[1]
    tm = _tile(rows_per_batch, 1024, SUBLANES)
    tn = _tile(d, 512, LANES)
    nj = d // tn
    return pl.pallas_call(
        _merge_kernel,
        out_shape=jax.ShapeDtypeStruct((m, d), BF16),
        grid=(m // tm, nj),
        in_specs=[pl.BlockSpec((tm, dr), lambda i, j: (i, 0)),
                  pl.BlockSpec((tm, dc), lambda i, j: (i, 0)),
                  pl.BlockSpec((dr, tn), lambda i, j: (0, j)),
                  pl.BlockSpec((dc, tn), lambda i, j: (0, j)),
                  pl.BlockSpec((tm, tn), lambda i, j: (i, j)),
                  pl.BlockSpec((tm, tn), lambda i, j: (i, nj + j))],
        out_specs=pl.BlockSpec((tm, tn), lambda i, j: (i, j)),
        compiler_params=_params("arbitrary", "arbitrary"),
        name="merge",
    )(o, ycb, wa, wb, sg, sg)


def _resid_kernel(a_ref, w_ref, x_ref, gt_ref, o_ref):
    y = jnp.dot(a_ref[...], w_ref[...], preferred_element_type=F32)
    o_ref[...] = x_ref[...] + gt_ref[...] * y


def _resid(a, w, x2d, mod3, gate_idx, rows_per_batch, tm_pref, name):
    m, kd = a.shape
    d = w.shape[1]
    tm = _tile(rows_per_batch, tm_pref, SUBLANES)
    tn = _tile(d, 512, LANES)
    tpb = rows_per_batch // tm
    return pl.pallas_call(
        _resid_kernel,
        out_shape=jax.ShapeDtypeStruct((m, d), F32),
        grid=(m // tm, d // tn),
        in_specs=[pl.BlockSpec((tm, kd), lambda i, j: (i, 0)),
                  pl.BlockSpec((kd, tn), lambda i, j: (0, j)),
                  pl.BlockSpec((tm, tn), lambda i, j: (i, j)),
                  pl.BlockSpec((None, 1, tn), lambda i, j: ((i // tpb) * N_ADA + gate_idx, 0, j))],
        out_specs=pl.BlockSpec((tm, tn), lambda i, j: (i, j)),
        compiler_params=_params("arbitrary", "arbitrary"),
        name=name,
    )(a, w, x2d, mod3)


def _ffn_up_kernel(h_ref, wg_ref, wv_ref, cw_ref, o_ref, carry_ref, *, tpb):
    i, j = pl.program_id(0), pl.program_id(1)
    tm = o_ref.shape[0]
    h = h_ref[...]
    gate = jnp.dot(h, wg_ref[...], preferred_element_type=F32)
    prev8 = _carry_swap(carry_ref, j, gate[tm - SUBLANES:], (i % tpb) == 0)
    gc = _causal_conv3(gate, prev8, cw_ref[...])
    val = jnp.dot(h, wv_ref[...], preferred_element_type=F32)
    o_ref[...] = (gc * jax.nn.sigmoid(gc) * val).astype(o_ref.dtype)


def _ffn_up(h, w, cw, rows_per_batch):
    m, d = h.shape
    dff = cw.shape[1]
    tm = _tile(rows_per_batch, 1024, SUBLANES)
    tc = _tile(dff, 256, LANES)
    ncb = dff // tc
    return pl.pallas_call(
        functools.partial(_ffn_up_kernel, tpb=rows_per_batch // tm),
        out_shape=jax.ShapeDtypeStruct((m, dff), BF16),
        grid=(m // tm, ncb),
        in_specs=[pl.BlockSpec((tm, d), lambda i, j: (i, 0)),
                  pl.BlockSpec((d, tc), lambda i, j: (0, j)),
                  pl.BlockSpec((d, tc), lambda i, j: (0, ncb + j)),
                  pl.BlockSpec((CONV_WIDTH, tc), lambda i, j: (0, j))],
        out_specs=pl.BlockSpec((tm, tc), lambda i, j: (i, j)),
        scratch_shapes=[pltpu.VMEM((ncb, SUBLANES, tc), F32)],
        compiler_params=_params("arbitrary", "arbitrary"),
        name="ffn_up",
    )(h, w, w, cw)


def kernel(x, c, w_ada, b_ada, norm1_gain, w_in, mu_shift, w0, a0, k_k, k_a, r_k, w_lora_decay,
           w_lora_iclr, w_lora_gate, lnx_w, lnx_b, conv_w_mix, w_o_rwkv, w_o_conv, w_out, norm2_gain,
           w_ffn_up, conv_w_ffn, w_ffn_down, final_gain):
    nb, seq, d = x.shape
    m = nb * seq
    depth = w_ada.shape[0]
    dr = w_o_rwkv.shape[1]
    dc = w_o_conv.shape[1]
    n_dec, n_icl, n_gat = w_lora_decay.shape[1], w_lora_iclr.shape[1], w_lora_gate.shape[1]
    n_lora = n_dec + n_icl + n_gat
    n_shift = 3 * dr + n_lora
    assert seq % CHUNK == 0 and dr % (2 * HEAD_DIM) == 0

    tn_a = _tile(dr, 512, LANES)
    lp = _round_up(n_lora, tn_a)
    assert (3 * dr) % lp == 0

    x2d = x.reshape(m, d)
    for layer in range(depth):
        wi = w_in[layer]
        w_a = jnp.concatenate(
            [wi[:, :3 * dr], jnp.pad(wi[:, 3 * dr:n_shift], ((0, 0), (0, lp - n_lora)))], axis=1).astype(BF16)
        mu_a = jnp.concatenate([mu_shift[layer][:3 * dr], jnp.pad(mu_shift[layer][3 * dr:], (0, lp - n_lora))])
        w_c = wi[:, n_shift:n_shift + 3 * dc].astype(BF16)
        w_g = wi[:, n_shift + 3 * dc:].astype(BF16)
        w_l = jnp.zeros((lp, 3 * dr), F32)
        w_l = w_l.at[:n_dec, :dr].set(w_lora_decay[layer])
        w_l = w_l.at[n_dec:n_dec + n_icl, dr:2 * dr].set(w_lora_iclr[layer])
        w_l = w_l.at[n_dec + n_icl:n_lora, 2 * dr:].set(w_lora_gate[layer]).astype(BF16)
        pv_l = jnp.concatenate([w0[layer], a0[layer], jnp.zeros((dr,), F32)]).reshape(1, 3 * dr)
        row = lambda p: p.reshape(1, dr)

        mod3 = _ada(c, w_ada[layer], b_ada[layer]).reshape(nb * N_ADA, 1, d)
        h = _norm_mod(x2d, norm1_gain[layer], mod3, 0, 1, seq)
        ps = _proj_shift(h, w_a, mu_a.reshape(1, -1), seq, 3 * dr // tn_a, tn_a, n_dec, n_icl)
        ycb = _proj_conv(h, w_c, conv_w_mix[layer], seq)
        sg = _proj_gate(h, w_g, seq)
        lig = _lora(ps, w_l, pv_l, seq, lp, 3 * dr // lp, dr)
        o = _scan(ps, lig, row(k_k[layer]), row(k_a[layer]), row(r_k[layer]), row(lnx_w[layer]),
                  row(lnx_b[layer]), nb, seq, dr)
        merged = _merge(o, ycb, w_o_rwkv[layer].astype(BF16), w_o_conv[layer].astype(BF16), sg, seq)
        x1 = _resid(merged, w_out[layer].astype(BF16), x2d, mod3, 2, seq, 1024, "attn_out")
        h2 = _norm_mod(x1, norm2_gain[layer], mod3, 3, 4, seq)
        act = _ffn_up(h2, w_ffn_up[layer].astype(BF16), conv_w_ffn[layer], seq)
        x2d = _resid(act, w_ffn_down[layer].astype(BF16), x1, mod3, 5, seq, 512, "ffn_down")
    return _final_norm(x2d, final_gain).reshape(nb, seq, d)
```

```python
import functools

import jax
import jax.numpy as jnp
from jax import lax
from jax.experimental import pallas as pl
from jax.experimental.pallas import tpu as pltpu

F32 = jnp.float32
BF16 = jnp.bfloat16

NORM_EPS = 1e-6
LNX_EPS = 64e-5
HEAD_DIM = 64
N_ADA = 6
CONV_WIDTH = 3
LANES = 128
SUBLANES = 8
CHUNK = 64
VMEM_LIMIT = 56 * 1024 * 1024
LHS_PAD = LANES

_NN = (((1,), (0,)), ((), ()))
_NT = (((1,), (1,)), ((), ()))
_TN = (((0,), (0,)), ((), ()))


def _round_up(n, m):
    return (n + m - 1) // m * m


def _tile(n, pref, quantum):
    t = min(pref, n) // quantum * quantum
    while t > quantum and n % t:
        t -= quantum
    assert t > 0 and n % t == 0, (n, pref, quantum)
    return t


def _params(*sem):
    return pltpu.CompilerParams(dimension_semantics=sem, vmem_limit_bytes=VMEM_LIMIT)


def _ada_kernel(cb_ref, w_ref, b_ref, o_ref, *, nb, kd, tn):
    rep = tn // LANES

    def body(kc, accs):
        k0 = pl.multiple_of(kc * SUBLANES, SUBLANES)
        w = w_ref[pl.ds(k0, SUBLANES), :]
        new = []
        for b in range(nb):
            cv = cb_ref[b, pl.ds(k0, SUBLANES), :]
            cv = cv * jax.nn.sigmoid(cv)
            new.append(accs[b] + w * jnp.concatenate([cv] * rep, axis=1))
        return tuple(new)

    init = tuple(jnp.zeros((SUBLANES, tn), F32) for _ in range(nb))
    accs = lax.fori_loop(0, kd // SUBLANES, body, init, unroll=8)
    rows = [jnp.sum(a, axis=0, keepdims=True) for a in accs]
    o_ref[...] = jnp.concatenate(rows, axis=0) + b_ref[...]


def _ada(c, w, b):
    nb, kd = c.shape
    n = w.shape[1]
    tn = _tile(n, 512, LANES)
    cb = jnp.broadcast_to(c[:, :, None], (nb, kd, LANES))
    return pl.pallas_call(
        functools.partial(_ada_kernel, nb=nb, kd=kd, tn=tn),
        out_shape=jax.ShapeDtypeStruct((nb, n), F32),
        grid=(n // tn,),
        in_specs=[pl.BlockSpec((nb, kd, LANES), lambda j: (0, 0, 0)),
                  pl.BlockSpec((kd, tn), lambda j: (0, j)),
                  pl.BlockSpec((1, tn), lambda j: (0, j))],
        out_specs=pl.BlockSpec((nb, tn), lambda j: (0, j)),
        compiler_params=_params("arbitrary"),
        name="ada",
    )(cb, w, b.reshape(1, n))


def _rms(x):
    return x * lax.rsqrt(jnp.mean(x * x, axis=-1, keepdims=True) + NORM_EPS)


def _norm_mod_kernel(x_ref, g_ref, sh_ref, sc_ref, o_ref):
    y = _rms(x_ref[...]) * g_ref[...]
    d = x_ref.shape[1]
    o_ref[:, :d] = (y * (1.0 + sc_ref[...]) + sh_ref[...]).astype(o_ref.dtype)
    o_ref[:, d:] = jnp.zeros((o_ref.shape[0], o_ref.shape[1] - d), o_ref.dtype)


def _norm_mod(x2d, gain, mod3, shift_idx, scale_idx, rows_per_batch):
    m, d = x2d.shape
    tm = _tile(rows_per_batch, 256, SUBLANES)
    tpb = rows_per_batch // tm
    return pl.pallas_call(
        _norm_mod_kernel,
        out_shape=jax.ShapeDtypeStruct((m, d + LHS_PAD), BF16),
        grid=(m // tm,),
        in_specs=[pl.BlockSpec((tm, d), lambda i: (i, 0)),
                  pl.BlockSpec((1, d), lambda i: (0, 0)),
                  pl.BlockSpec((None, 1, d), lambda i: ((i // tpb) * N_ADA + shift_idx, 0, 0)),
                  pl.BlockSpec((None, 1, d), lambda i: ((i // tpb) * N_ADA + scale_idx, 0, 0))],
        out_specs=pl.BlockSpec((tm, d + LHS_PAD), lambda i: (i, 0)),
        compiler_params=_params("arbitrary"),
        name="norm_mod",
    )(x2d, gain.reshape(1, d), mod3, mod3)


def _final_norm_kernel(x_ref, g_ref, o_ref):
    o_ref[...] = _rms(x_ref[...]) * g_ref[...]


def _final_norm(x2d, gain):
    m, d = x2d.shape
    tm = _tile(m, 256, SUBLANES)
    return pl.pallas_call(
        _final_norm_kernel,
        out_shape=jax.ShapeDtypeStruct((m, d), F32),
        grid=(m // tm,),
        in_specs=[pl.BlockSpec((tm, d), lambda i: (i, 0)),
                  pl.BlockSpec((1, d), lambda i: (0, 0))],
        out_specs=pl.BlockSpec((tm, d), lambda i: (i, 0)),
        compiler_params=_params("arbitrary"),
        name="final_norm",
    )(x2d, gain.reshape(1, d))


def _shift_rows(p, prev8, s):
    rolled = pltpu.roll(p, s, 0)
    row = lax.broadcasted_iota(jnp.int32, prev8.shape, 0)
    head = jnp.where(row < s, pltpu.roll(prev8, s, 0), rolled[:SUBLANES])
    return jnp.concatenate([head, rolled[SUBLANES:]], axis=0)


def _carry_swap(carry_ref, j, tail, first):
    prev8 = jnp.where(first, 0.0, carry_ref[j])
    carry_ref[j] = tail
    return prev8


def _causal_conv3(z, prev8, cw):
    return cw[0:1] * _shift_rows(z, prev8, 2) + cw[1:2] * _shift_rows(z, prev8, 1) + cw[2:3] * z


def _softplus(x):
    return jnp.maximum(x, 0.0) + jnp.log(1.0 + jnp.exp(-jnp.abs(x)))


def _proj_shift_kernel(h_ref, w_ref, mu_ref, o_ref, carry_ref, *, tpb, j_lora, n_tanh, n_lin):
    i, j = pl.program_id(0), pl.program_id(1)
    tm, tn = o_ref.shape
    p = jnp.dot(h_ref[:, :w_ref.shape[0]], w_ref[...], preferred_element_type=F32)
    prev8 = _carry_swap(carry_ref, j, p[tm - SUBLANES:], (i % tpb) == 0)
    out = p + (_shift_rows(p, prev8, 1) - p) * mu_ref[...]

    @pl.when(j < j_lora)
    def _():
        o_ref[...] = out

    @pl.when(j >= j_lora)
    def _():
        col = (j - j_lora) * tn + lax.broadcasted_iota(jnp.int32, (1, tn), 1)
        act = jnp.where(col < n_tanh, jnp.tanh(out),
                        jnp.where(col < n_tanh + n_lin, out, jax.nn.sigmoid(out)))
        o_ref[...] = act


def _proj_shift(h, w, mu, rows_per_batch, j_lora, tn, n_tanh, n_lin):
    m, dp = h.shape
    d = w.shape[0]
    n = w.shape[1]
    tm = _tile(rows_per_batch, 1024, SUBLANES)
    return pl.pallas_call(
        functools.partial(_proj_shift_kernel, tpb=rows_per_batch // tm, j_lora=j_lora,
                          n_tanh=n_tanh, n_lin=n_lin),
        out_shape=jax.ShapeDtypeStruct((m, n), F32),
        grid=(m // tm, n // tn),
        in_specs=[pl.BlockSpec((tm, dp), lambda i, j: (i, 0)),
                  pl.BlockSpec((d, tn), lambda i, j: (0, j)),
                  pl.BlockSpec((1, tn), lambda i, j: (0, j))],
        out_specs=pl.BlockSpec((tm, tn), lambda i, j: (i, j)),
        scratch_shapes=[pltpu.VMEM((n // tn, SUBLANES, tn), F32)],
        compiler_params=_params("arbitrary", "arbitrary"),
        name="proj_shift",
    )(h, w, mu)


def _proj_conv_kernel(h_ref, wb_ref, wc_ref, wx_ref, cw_ref, o_ref, carry_ref, *, tpb):
    i, j = pl.program_id(0), pl.program_id(1)
    tm = o_ref.shape[0]
    h = h_ref[:, :wb_ref.shape[0]]
    z = (jnp.dot(h, wc_ref[...], preferred_element_type=F32)
         * jnp.dot(h, wx_ref[...], preferred_element_type=F32))
    prev8 = _carry_swap(carry_ref, j, z[tm - SUBLANES:], (i % tpb) == 0)
    conv = _causal_conv3(z, prev8, cw_ref[...])
    o_ref[...] = (jnp.dot(h, wb_ref[...], preferred_element_type=F32) * conv).astype(o_ref.dtype)


def _proj_conv(h, w, cw, rows_per_batch):
    m, dp = h.shape
    d = w.shape[0]
    dc = cw.shape[1]
    tm = _tile(rows_per_batch, 1024, SUBLANES)
    tc = _tile(dc, 512, LANES)
    ncb = dc // tc
    wspec = lambda grp: pl.BlockSpec((d, tc), lambda i, j: (0, grp * ncb + j))
    return pl.pallas_call(
        functools.partial(_proj_conv_kernel, tpb=rows_per_batch // tm),
        out_shape=jax.ShapeDtypeStruct((m, dc), BF16),
        grid=(m // tm, ncb),
        in_specs=[pl.BlockSpec((tm, dp), lambda i, j: (i, 0)), wspec(0), wspec(1), wspec(2),
                  pl.BlockSpec((CONV_WIDTH, tc), lambda i, j: (0, j))],
        out_specs=pl.BlockSpec((tm, tc), lambda i, j: (i, j)),
        scratch_shapes=[pltpu.VMEM((ncb, SUBLANES, tc), F32)],
        compiler_params=_params("arbitrary", "arbitrary"),
        name="proj_conv",
    )(h, w, w, w, cw)


def _proj_gate_kernel(h_ref, w_ref, o_ref):
    p = jnp.dot(h_ref[:, :w_ref.shape[0]], w_ref[...], preferred_element_type=F32)
    o_ref[...] = jax.nn.sigmoid(p).astype(o_ref.dtype)


def _proj_gate(h, w, rows_per_batch):
    m, dp = h.shape
    d = w.shape[0]
    n = w.shape[1]
    tm = _tile(rows_per_batch, 1024, SUBLANES)
    tn = _tile(n, 1024, LANES)
    return pl.pallas_call(
        _proj_gate_kernel,
        out_shape=jax.ShapeDtypeStruct((m, n), BF16),
        grid=(m // tm, n // tn),
        in_specs=[pl.BlockSpec((tm, dp), lambda i, j: (i, 0)),
                  pl.BlockSpec((d, tn), lambda i, j: (0, j))],
        out_specs=pl.BlockSpec((tm, tn), lambda i, j: (i, j)),
        compiler_params=_params("arbitrary", "arbitrary"),
        name="proj_gate",
    )(h, w)


def _lora_kernel(a_ref, w_ref, pv_ref, o_ref, *, tiles_per_group):
    grp = pl.program_id(1) // tiles_per_group
    pre = jnp.dot(a_ref[...].astype(BF16), w_ref[...], preferred_element_type=F32) + pv_ref[...]

    @pl.when(grp == 0)
    def _():
        o_ref[...] = -jnp.exp(-_softplus(-pre) - 0.5)

    @pl.when(grp == 1)
    def _():
        o_ref[...] = jax.nn.sigmoid(pre)

    @pl.when(grp == 2)
    def _():
        o_ref[...] = pre


def _lora(ps, w, pv, rows_per_batch, lp, col_block, dr):
    m = ps.shape[0]
    n = w.shape[1]
    tm = _tile(rows_per_batch, 1024, SUBLANES)
    tn = _tile(dr, 512, LANES)
    return pl.pallas_call(
        functools.partial(_lora_kernel, tiles_per_group=dr // tn),
        out_shape=jax.ShapeDtypeStruct((m, n), F32),
        grid=(m // tm, n // tn),
        in_specs=[pl.BlockSpec((tm, lp), lambda i, j: (i, col_block)),
                  pl.BlockSpec((lp, tn), lambda i, j: (0, j)),
                  pl.BlockSpec((1, tn), lambda i, j: (0, j))],
        out_specs=pl.BlockSpec((tm, tn), lambda i, j: (i, j)),
        compiler_params=_params("arbitrary", "arbitrary"),
        name="lora",
    )(ps, w, pv)


def _split(x, n):
    pieces = []
    for _ in range(n):
        p = x.astype(BF16)
        pieces.append(p)
        x = x - p.astype(F32)
    return pieces


def _dot(a, b, dims, na=1, nb=1):
    pa = [a] if a.dtype == BF16 else _split(a, na)
    pb = [b] if b.dtype == BF16 else _split(b, nb)
    depth = max(len(pa), len(pb))
    acc = None
    for ia, xa in enumerate(pa):
        for ib, xb in enumerate(pb):
            if ia + ib >= depth:
                continue
            d = lax.dot_general(xa, xb, dims, preferred_element_type=F32)
            acc = d if acc is None else acc + d
    return acc


def _scan_kernel(r_ref, k_ref, v_ref, lw_ref, ic_ref, g_ref, kk_ref, ka_ref, rk_ref, lnw_ref, lnb_ref,
                 o_ref, st_ref, *, nb, pairs):
    cl = CHUNK
    pw = 2 * HEAD_DIM
    chains = [(b, pp) for b in range(nb) for pp in range(pairs)]
    nch = len(chains)

    @pl.when(pl.program_id(1) == 0)
    def _():
        st_ref[...] = jnp.zeros_like(st_ref)

    ri = lax.broadcasted_iota(jnp.int32, (2 * cl, 2 * cl), 0)
    ci = lax.broadcasted_iota(jnp.int32, (2 * cl, 2 * cl), 1)
    strict = ri > ci
    incl = ri >= ci
    eye = jnp.where(ri == ci, 1.0, 0.0).astype(F32)
    hi = lax.broadcasted_iota(jnp.int32, (pw, pw), 0) // HEAD_DIM
    hj = lax.broadcasted_iota(jnp.int32, (pw, pw), 1) // HEAD_DIM
    head_ones = jnp.where(hi == hj, 1.0, 0.0).astype(BF16)
    ti = lax.broadcasted_iota(jnp.int32, (cl, cl), 0)
    tj = lax.broadcasted_iota(jnp.int32, (cl, cl), 1)
    cum_ones = jnp.where(ti >= tj, 1.0, 0.0).astype(BF16)
    first_head = lax.broadcasted_iota(jnp.int32, (1, pw), 1) < HEAD_DIM
    inv_n = 1.0 / HEAD_DIM

    def block_diag(x):
        return jnp.concatenate([jnp.where(first_head, x, 0.0), jnp.where(first_head, 0.0, x)], axis=0)

    def head_sums(xs):
        s = _dot(jnp.concatenate(xs, axis=0), head_ones, _NN)
        return [s[i * cl:(i + 1) * cl] for i in range(nch)]

    def tok(ref):
        return [ref[b, :, pp * pw:(pp + 1) * pw] for b, pp in chains]

    def par(ref):
        return [ref[:, pp * pw:(pp + 1) * pw] for _, pp in chains]

    def each(f, *ls):
        return [f(*xs) for xs in zip(*ls)]

    r, kraw, v, lw, ic, g = tok(r_ref), tok(k_ref), tok(v_ref), tok(lw_ref), tok(ic_ref), tok(g_ref)

    kk = each(lambda k, p: k * p, kraw, par(kk_ref))
    ss = head_sums(each(lambda x: x * x, kk))
    kk = each(lambda x, s: x / jnp.maximum(jnp.sqrt(s), 1e-12), kk, ss)
    kmod = each(lambda k, i, p: k * (1.0 + (i - 1.0) * p), kraw, ic, par(ka_ref))
    bvec = each(lambda x, i: x * i, kk, ic)

    lg_b = [_dot(cum_ones, lw_ref[b], _NN, nb=2) for b in range(nb)]
    lg = [lg_b[b][:, pp * pw:(pp + 1) * pw] for b, pp in chains]
    gcol = each(lambda x: jnp.transpose(jnp.broadcast_to(jnp.exp(x[cl - 1:cl, :]), (pw, pw))), lg)
    g_inv = each(lambda x: jnp.exp(-x), lg)
    g_end = each(lambda x: jnp.exp(x[cl - 1:cl, :] - x), lg)

    a_bd = each(lambda x, l, w: block_diag(-x * jnp.exp(l - w)), kk, lg, lw)
    r_bd = each(lambda x, l: block_diag(x * jnp.exp(l)), r, lg)
    b_bd = each(lambda x, gi: block_diag(x * gi), bvec, g_inv)
    k_bd = each(lambda x, gi: block_diag(x * gi), kmod, g_inv)
    v_bd = each(block_diag, v)
    bend_bd = each(lambda x, ge: block_diag(x * ge), bvec, g_end)
    kend_bd = each(lambda x, ge: block_diag(x * ge), kmod, g_end)

    sc = each(lambda a, rr, b, k: _dot(jnp.concatenate([a, rr], axis=0), jnp.concatenate([b, k], axis=0), _NT),
              a_bd, r_bd, b_bd, k_bd)
    l_ab = each(lambda s: jnp.where(strict, s[:2 * cl, :2 * cl], 0.0), sc)
    m_ak = each(lambda s: jnp.where(strict, s[:2 * cl, 2 * cl:], 0.0), sc)
    m_rb = each(lambda s: jnp.where(incl, s[2 * cl:, :2 * cl], 0.0), sc)
    m_rk = each(lambda s: jnp.where(incl, s[2 * cl:, 2 * cl:], 0.0), sc)

    tinv = each(lambda l: eye + l, l_ab)
    lpow = l_ab
    for _ in range(cl.bit_length() - 2):
        lpow = each(lambda p: _dot(p, p, _NN), lpow)
        tinv = each(lambda t, p: t + _dot(t, p, _NN), tinv, lpow)

    st = [st_ref[i] for i in range(nch)]
    mv = each(lambda mm, vv: _dot(mm, vv, _NN), m_ak, v_bd)
    rhs = each(lambda a, s, x: _dot(a, s, _NN) + x, a_bd, st, mv)
    u = each(lambda t, x: _dot(t, x, _NN), tinv, rhs)
    y0 = each(lambda rr, s, mm, vv: _dot(rr, s, _NN) + _dot(mm, vv, _NN), r_bd, st, m_rk, v_bd)
    y_bd = each(lambda mm, uu, y: _dot(mm, uu, _NN) + y, m_rb, u, y0)
    for i in range(nch):
        st_ref[i] = gcol[i] * st[i] + _dot(bend_bd[i], u[i], _TN) + _dot(kend_bd[i], v_bd[i], _TN)
    y = each(lambda x: x[:cl] + x[cl:], y_bd)

    mean = each(lambda s: s * inv_n, head_sums(y))
    dy = each(lambda a, b: a - b, y, mean)
    var = each(lambda s: s * inv_n, head_sums(each(lambda x: x * x, dy)))
    bon = head_sums(each(lambda a, b, p: a * b * p, r, kmod, par(rk_ref)))
    for i, (b, pp) in enumerate(chains):
        sl = slice(pp * pw, (pp + 1) * pw)
        yn = dy[i] * lax.rsqrt(var[i] + LNX_EPS) * lnw_ref[:, sl] + lnb_ref[:, sl]
        o_ref[b, :, sl] = ((yn + bon[i] * v[i]) * g[i]).astype(o_ref.dtype)


def _scan(ps, lig, k_k, k_a, r_k, lnx_w, lnx_b, nb, seq, dr):
    pw = 2 * HEAD_DIM
    pairs = 4 if dr % (4 * pw) == 0 else 1
    bw = pairs * pw
    npb = dr // bw
    nc = seq // CHUNK
    ps3 = ps.reshape(nb, seq, ps.shape[1])
    lig3 = lig.reshape(nb, seq, lig.shape[1])
    tok = lambda off: pl.BlockSpec((nb, CHUNK, bw), lambda p, c: (0, c, off * npb + p))
    par = pl.BlockSpec((1, bw), lambda p, c: (0, p))
    o = pl.pallas_call(
        functools.partial(_scan_kernel, nb=nb, pairs=pairs),
        out_shape=jax.ShapeDtypeStruct((nb, seq, dr), BF16),
        grid=(npb, nc),
        in_specs=[tok(0), tok(1), tok(2), tok(0), tok(1), tok(2), par, par, par, par, par],
        out_specs=pl.BlockSpec((nb, CHUNK, bw), lambda p, c: (0, c, p)),
        scratch_shapes=[pltpu.VMEM((nb * pairs, pw, pw), F32)],
        compiler_params=_params("arbitrary", "arbitrary"),
        name="rwkv_scan",
    )(ps3, ps3, ps3, lig3, lig3, lig3, k_k, k_a, r_k, lnx_w, lnx_b)
    return o.reshape(nb * seq, dr)


def _merge_kernel(o_ref, y_ref, wa_ref, wb_ref, sa_ref, sb_ref, out_ref):
    ya = jnp.dot(o_ref[...], wa_ref[...], preferred_element_type=F32)
    yb = jnp.dot(y_ref[...], wb_ref[...], preferred_element_type=F32)
    out_ref[...] = (sa_ref[...].astype(F32) * ya + sb_ref[...].astype(F32) * yb).astype(out_ref.dtype)


def _merge(o, ycb, wa, wb, sg, rows_per_batch):
    m, dr = o.shape
    dc = ycb.shape[1]
    d = wa.shape[1]
    tm = _tile(rows_per_batch, 1024, SUBLANES)
    tn = _tile(d, 512, LANES)
    nj = d // tn
    return pl.pallas_call(
        _merge_kernel,
        out_shape=jax.ShapeDtypeStruct((m, d), BF16),
        grid=(m // tm, nj),
        in_specs=[pl.BlockSpec((tm, dr), lambda i, j: (i, 0)),
                  pl.BlockSpec((tm, dc), lambda i, j: (i, 0)),
                  pl.BlockSpec((dr, tn), lambda i, j: (0, j)),
                  pl.BlockSpec((dc, tn), lambda i, j: (0, j)),
                  pl.BlockSpec((tm, tn), lambda i, j: (i, j)),
                  pl.BlockSpec((tm, tn), lambda i, j: (i, nj + j))],
        out_specs=pl.BlockSpec((tm, tn), lambda i, j: (i, j)),
        compiler_params=_params("arbitrary", "arbitrary"),
        name="merge",
    )(o, ycb, wa, wb, sg, sg)


def _resid_kernel(a_ref, w_ref, x_ref, gt_ref, o_ref):
    y = jnp.dot(a_ref[...], w_ref[...], preferred_element_type=F32)
    o_ref[...] = x_ref[...] + gt_ref[...] * y


def _resid(a, w, x2d, mod3, gate_idx, rows_per_batch, tm_pref, name):
    m, kd = a.shape
    d = w.shape[1]
    tm = _tile(rows_per_batch, tm_pref, SUBLANES)
    tn = _tile(d, 512, LANES)
    tpb = rows_per_batch // tm
    return pl.pallas_call(
        _resid_kernel,
        out_shape=jax.ShapeDtypeStruct((m, d), F32),
        grid=(m // tm, d // tn),
        in_specs=[pl.BlockSpec((tm, kd), lambda i, j: (i, 0)),
                  pl.BlockSpec((kd, tn), lambda i, j: (0, j)),
                  pl.BlockSpec((tm, tn), lambda i, j: (i, j)),
                  pl.BlockSpec((None, 1, tn), lambda i, j: ((i // tpb) * N_ADA + gate_idx, 0, j))],
        out_specs=pl.BlockSpec((tm, tn), lambda i, j: (i, j)),
        compiler_params=_params("arbitrary", "arbitrary"),
        name=name,
    )(a, w, x2d, mod3)


def _ffn_up_kernel(h_ref, wg_ref, wv_ref, cw_ref, o_ref, carry_ref, *, tpb, n_sub):
    i, j = pl.program_id(0), pl.program_id(1)
    ts = o_ref.shape[0] // n_sub
    cw = cw_ref[...]

    def finish(s, gate, val, prev8):
        gc = _causal_conv3(gate, prev8, cw)
        o_ref[s * ts:(s + 1) * ts, :] = (gc * jax.nn.sigmoid(gc) * val).astype(o_ref.dtype)

    prev8 = jnp.where((i % tpb) == 0, 0.0, carry_ref[j])
    pending = None
    for s in range(n_sub):
        h = h_ref[s * ts:(s + 1) * ts, :wg_ref.shape[0]]
        gate = jnp.dot(h, wg_ref[...], preferred_element_type=F32)
        val = jnp.dot(h, wv_ref[...], preferred_element_type=F32)
        if pending is not None:
            finish(*pending)
        pending = (s, gate, val, prev8)
        prev8 = gate[ts - SUBLANES:]
    finish(*pending)
    carry_ref[j] = prev8


def _ffn_up(h, w, cw, rows_per_batch):
    m, dp = h.shape
    d = w.shape[0]
    dff = cw.shape[1]
    tm = _tile(rows_per_batch, 1024, SUBLANES)
    tc = _tile(dff, 256, LANES)
    ncb = dff // tc
    return pl.pallas_call(
        functools.partial(_ffn_up_kernel, tpb=rows_per_batch // tm, n_sub=8 if tm % 64 == 0 else 1),
        out_shape=jax.ShapeDtypeStruct((m, dff), BF16),
        grid=(m // tm, ncb),
        in_specs=[pl.BlockSpec((tm, dp), lambda i, j: (i, 0)),
                  pl.BlockSpec((d, tc), lambda i, j: (0, j)),
                  pl.BlockSpec((d, tc), lambda i, j: (0, ncb + j)),
                  pl.BlockSpec((CONV_WIDTH, tc), lambda i, j: (0, j))],
        out_specs=pl.BlockSpec((tm, tc), lambda i, j: (i, j)),
        scratch_shapes=[pltpu.VMEM((ncb, SUBLANES, tc), F32)],
        compiler_params=_params("arbitrary", "arbitrary"),
        name="ffn_up",
    )(h, w, w, cw)


def kernel(x, c, w_ada, b_ada, norm1_gain, w_in, mu_shift, w0, a0, k_k, k_a, r_k, w_lora_decay,
           w_lora_iclr, w_lora_gate, lnx_w, lnx_b, conv_w_mix, w_o_rwkv, w_o_conv, w_out, norm2_gain,
           w_ffn_up, conv_w_ffn, w_ffn_down, final_gain):
    nb, seq, d = x.shape
    m = nb * seq
    depth = w_ada.shape[0]
    dr = w_o_rwkv.shape[1]
    dc = w_o_conv.shape[1]
    n_dec, n_icl, n_gat = w_lora_decay.shape[1], w_lora_iclr.shape[1], w_lora_gate.shape[1]
    n_lora = n_dec + n_icl + n_gat
    n_shift = 3 * dr + n_lora
    assert seq % CHUNK == 0 and dr % (2 * HEAD_DIM) == 0

    tn_a = _tile(dr, 512, LANES)
    lp = _round_up(n_lora, tn_a)
    assert (3 * dr) % lp == 0

    x2d = x.reshape(m, d)
    for layer in range(depth):
        wi = w_in[layer]
        w_a = jnp.concatenate(
            [wi[:, :3 * dr], jnp.pad(wi[:, 3 * dr:n_shift], ((0, 0), (0, lp - n_lora)))], axis=1).astype(BF16)
        mu_a = jnp.concatenate([mu_shift[layer][:3 * dr], jnp.pad(mu_shift[layer][3 * dr:], (0, lp - n_lora))])
        w_c = wi[:, n_shift:n_shift + 3 * dc].astype(BF16)
        w_g = wi[:, n_shift + 3 * dc:].astype(BF16)
        w_l = jnp.zeros((lp, 3 * dr), F32)
        w_l = w_l.at[:n_dec, :dr].set(w_lora_decay[layer])
        w_l = w_l.at[n_dec:n_dec + n_icl, dr:2 * dr].set(w_lora_iclr[layer])
        w_l = w_l.at[n_dec + n_icl:n_lora, 2 * dr:].set(w_lora_gate[layer]).astype(BF16)
        pv_l = jnp.concatenate([w0[layer], a0[layer], jnp.zeros((dr,), F32)]).reshape(1, 3 * dr)
        row = lambda p: p.reshape(1, dr)

        mod3 = _ada(c, w_ada[layer], b_ada[layer]).reshape(nb * N_ADA, 1, d)
        h = _norm_mod(x2d, norm1_gain[layer], mod3, 0, 1, seq)
        ps = _proj_shift(h, w_a, mu_a.reshape(1, -1), seq, 3 * dr // tn_a, tn_a, n_dec, n_icl)
        ycb = _proj_conv(h, w_c, conv_w_mix[layer], seq)
        sg = _proj_gate(h, w_g, seq)
        lig = _lora(ps, w_l, pv_l, seq, lp, 3 * dr // lp, dr)
        o = _scan(ps, lig, row(k_k[layer]), row(k_a[layer]), row(r_k[layer]), row(lnx_w[layer]),
                  row(lnx_b[layer]), nb, seq, dr)
        merged = _merge(o, ycb, w_o_rwkv[layer].astype(BF16), w_o_conv[layer].astype(BF16), sg, seq)
        x1 = _resid(merged, w_out[layer].astype(BF16), x2d, mod3, 2, seq, 1024, "attn_out")
        h2 = _norm_mod(x1, norm2_gain[layer], mod3, 3, 4, seq)
        act = _ffn_up(h2, w_ffn_up[layer].astype(BF16), conv_w_ffn[layer], seq)
        x2d = _resid(act, w_ffn_down[layer].astype(BF16), x1, mod3, 5, seq, 512, "ffn_down")
    return _final_norm(x2d, final_gain).reshape(nb, seq, d)
```

```python
import functools

import jax
import jax.numpy as jnp
from jax import lax
from jax.experimental import pallas as pl
from jax.experimental.pallas import tpu as pltpu

F32 = jnp.float32
BF16 = jnp.bfloat16

NORM_EPS = 1e-6
LNX_EPS = 64e-5
HEAD_DIM = 64
N_ADA = 6
CONV_WIDTH = 3
LANES = 128
SUBLANES = 8
CHUNK = 64
VMEM_LIMIT = 56 * 1024 * 1024
LHS_PAD = LANES

_NN = (((1,), (0,)), ((), ()))
_NT = (((1,), (1,)), ((), ()))
_TN = (((0,), (0,)), ((), ()))


def _round_up(n, m):
    return (n + m - 1) // m * m


def _tile(n, pref, quantum):
    t = min(pref, n) // quantum * quantum
    while t > quantum and n % t:
        t -= quantum
    assert t > 0 and n % t == 0, (n, pref, quantum)
    return t


def _params(*sem):
    return pltpu.CompilerParams(dimension_semantics=sem, vmem_limit_bytes=VMEM_LIMIT)


def _ada_kernel(cb_ref, w_ref, b_ref, o_ref, *, nb, kd, tn):
    rep = tn // LANES

    def body(kc, accs):
        k0 = pl.multiple_of(kc * SUBLANES, SUBLANES)
        w = w_ref[pl.ds(k0, SUBLANES), :]
        new = []
        for b in range(nb):
            cv = cb_ref[b, pl.ds(k0, SUBLANES), :]
            cv = cv * jax.nn.sigmoid(cv)
            new.append(accs[b] + w * jnp.concatenate([cv] * rep, axis=1))
        return tuple(new)

    init = tuple(jnp.zeros((SUBLANES, tn), F32) for _ in range(nb))
    accs = lax.fori_loop(0, kd // SUBLANES, body, init, unroll=8)
    rows = [jnp.sum(a, axis=0, keepdims=True) for a in accs]
    o_ref[...] = jnp.concatenate(rows, axis=0) + b_ref[...]


def _ada(c, w, b):
    nb, kd = c.shape
    n = w.shape[1]
    tn = _tile(n, 512, LANES)
    cb = jnp.broadcast_to(c[:, :, None], (nb, kd, LANES))
    return pl.pallas_call(
        functools.partial(_ada_kernel, nb=nb, kd=kd, tn=tn),
        out_shape=jax.ShapeDtypeStruct((nb, n), F32),
        grid=(n // tn,),
        in_specs=[pl.BlockSpec((nb, kd, LANES), lambda j: (0, 0, 0)),
                  pl.BlockSpec((kd, tn), lambda j: (0, j)),
                  pl.BlockSpec((1, tn), lambda j: (0, j))],
        out_specs=pl.BlockSpec((nb, tn), lambda j: (0, j)),
        compiler_params=_params("arbitrary"),
        name="ada",
    )(cb, w, b.reshape(1, n))


def _rms(x):
    return x * lax.rsqrt(jnp.mean(x * x, axis=-1, keepdims=True) + NORM_EPS)


def _norm_mod_kernel(x_ref, g_ref, sh_ref, sc_ref, o_ref):
    y = _rms(x_ref[...]) * g_ref[...]
    d = x_ref.shape[1]
    o_ref[:, :d] = (y * (1.0 + sc_ref[...]) + sh_ref[...]).astype(o_ref.dtype)
    o_ref[:, d:] = jnp.zeros((o_ref.shape[0], o_ref.shape[1] - d), o_ref.dtype)


def _norm_mod(x2d, gain, mod3, shift_idx, scale_idx, rows_per_batch):
    m, d = x2d.shape
    tm = _tile(rows_per_batch, 256, SUBLANES)
    tpb = rows_per_batch // tm
    return pl.pallas_call(
        _norm_mod_kernel,
        out_shape=jax.ShapeDtypeStruct((m, d + LHS_PAD), BF16),
        grid=(m // tm,),
        in_specs=[pl.BlockSpec((tm, d), lambda i: (i, 0)),
                  pl.BlockSpec((1, d), lambda i: (0, 0)),
                  pl.BlockSpec((None, 1, d), lambda i: ((i // tpb) * N_ADA + shift_idx, 0, 0)),
                  pl.BlockSpec((None, 1, d), lambda i: ((i // tpb) * N_ADA + scale_idx, 0, 0))],
        out_specs=pl.BlockSpec((tm, d + LHS_PAD), lambda i: (i, 0)),
        compiler_params=_params("arbitrary"),
        name="norm_mod",
    )(x2d, gain.reshape(1, d), mod3, mod3)


def _final_norm_kernel(x_ref, g_ref, o_ref):
    o_ref[...] = _rms(x_ref[...]) * g_ref[...]


def _final_norm(x2d, gain):
    m, d = x2d.shape
    tm = _tile(m, 256, SUBLANES)
    return pl.pallas_call(
        _final_norm_kernel,
        out_shape=jax.ShapeDtypeStruct((m, d), F32),
        grid=(m // tm,),
        in_specs=[pl.BlockSpec((tm, d), lambda i: (i, 0)),
                  pl.BlockSpec((1, d), lambda i: (0, 0))],
        out_specs=pl.BlockSpec((tm, d), lambda i: (i, 0)),
        compiler_params=_params("arbitrary"),
        name="final_norm",
    )(x2d, gain.reshape(1, d))


def _shift_rows(p, prev8, s):
    rolled = pltpu.roll(p, s, 0)
    row = lax.broadcasted_iota(jnp.int32, prev8.shape, 0)
    head = jnp.where(row < s, pltpu.roll(prev8, s, 0), rolled[:SUBLANES])
    return jnp.concatenate([head, rolled[SUBLANES:]], axis=0)


def _carry_swap(carry_ref, j, tail, first):
    prev8 = jnp.where(first, 0.0, carry_ref[j])
    carry_ref[j] = tail
    return prev8


def _causal_conv3(z, prev8, cw):
    return cw[0:1] * _shift_rows(z, prev8, 2) + cw[1:2] * _shift_rows(z, prev8, 1) + cw[2:3] * z


def _softplus(x):
    return jnp.maximum(x, 0.0) + jnp.log(1.0 + jnp.exp(-jnp.abs(x)))


def _proj_shift_kernel(h_ref, w_ref, mu_ref, o_ref, carry_ref, *, tpb, j_lora, n_tanh, n_lin):
    i, j = pl.program_id(0), pl.program_id(1)
    tm, tn = o_ref.shape
    p = jnp.dot(h_ref[:, :w_ref.shape[0]], w_ref[...], preferred_element_type=F32)
    prev8 = _carry_swap(carry_ref, j, p[tm - SUBLANES:], (i % tpb) == 0)
    out = p + (_shift_rows(p, prev8, 1) - p) * mu_ref[...]

    @pl.when(j < j_lora)
    def _():
        o_ref[...] = out

    @pl.when(j >= j_lora)
    def _():
        col = (j - j_lora) * tn + lax.broadcasted_iota(jnp.int32, (1, tn), 1)
        act = jnp.where(col < n_tanh, jnp.tanh(out),
                        jnp.where(col < n_tanh + n_lin, out, jax.nn.sigmoid(out)))
        o_ref[...] = act


def _proj_shift(h, w, mu, rows_per_batch, j_lora, tn, n_tanh, n_lin):
    m, dp = h.shape
    d = w.shape[0]
    n = w.shape[1]
    tm = _tile(rows_per_batch, 1024, SUBLANES)
    return pl.pallas_call(
        functools.partial(_proj_shift_kernel, tpb=rows_per_batch // tm, j_lora=j_lora,
                          n_tanh=n_tanh, n_lin=n_lin),
        out_shape=jax.ShapeDtypeStruct((m, n), F32),
        grid=(m // tm, n // tn),
        in_specs=[pl.BlockSpec((tm, dp), lambda i, j: (i, 0)),
                  pl.BlockSpec((d, tn), lambda i, j: (0, j)),
                  pl.BlockSpec((1, tn), lambda i, j: (0, j))],
        out_specs=pl.BlockSpec((tm, tn), lambda i, j: (i, j)),
        scratch_shapes=[pltpu.VMEM((n // tn, SUBLANES, tn), F32)],
        compiler_params=_params("arbitrary", "arbitrary"),
        name="proj_shift",
    )(h, w, mu)


def _proj_conv_kernel(h_ref, wb_ref, wc_ref, wx_ref, cw_ref, o_ref, carry_ref, *, tpb):
    i, j = pl.program_id(0), pl.program_id(1)
    tm = o_ref.shape[0]
    h = h_ref[:, :wb_ref.shape[0]]
    z = (jnp.dot(h, wc_ref[...], preferred_element_type=F32)
         * jnp.dot(h, wx_ref[...], preferred_element_type=F32))
    prev8 = _carry_swap(carry_ref, j, z[tm - SUBLANES:], (i % tpb) == 0)
    conv = _causal_conv3(z, prev8, cw_ref[...])
    o_ref[...] = (jnp.dot(h, wb_ref[...], preferred_element_type=F32) * conv).astype(o_ref.dtype)


def _proj_conv(h, w, cw, rows_per_batch):
    m, dp = h.shape
    d = w.shape[0]
    dc = cw.shape[1]
    tm = _tile(rows_per_batch, 1024, SUBLANES)
    tc = _tile(dc, 512, LANES)
    ncb = dc // tc
    wspec = lambda grp: pl.BlockSpec((d, tc), lambda i, j: (0, grp * ncb + j))
    return pl.pallas_call(
        functools.partial(_proj_conv_kernel, tpb=rows_per_batch // tm),
        out_shape=jax.ShapeDtypeStruct((m, dc), BF16),
        grid=(m // tm, ncb),
        in_specs=[pl.BlockSpec((tm, dp), lambda i, j: (i, 0)), wspec(0), wspec(1), wspec(2),
                  pl.BlockSpec((CONV_WIDTH, tc), lambda i, j: (0, j))],
        out_specs=pl.BlockSpec((tm, tc), lambda i, j: (i, j)),
        scratch_shapes=[pltpu.VMEM((ncb, SUBLANES, tc), F32)],
        compiler_params=_params("arbitrary", "arbitrary"),
        name="proj_conv",
    )(h, w, w, w, cw)


def _proj_gate_kernel(h_ref, w_ref, o_ref):
    p = jnp.dot(h_ref[:, :w_ref.shape[0]], w_ref[...], preferred_element_type=F32)
    o_ref[...] = jax.nn.sigmoid(p).astype(o_ref.dtype)


def _proj_gate(h, w, rows_per_batch):
    m, dp = h.shape
    d = w.shape[0]
    n = w.shape[1]
    tm = _tile(rows_per_batch, 1024, SUBLANES)
    tn = _tile(n, 1024, LANES)
    return pl.pallas_call(
        _proj_gate_kernel,
        out_shape=jax.ShapeDtypeStruct((m, n), BF16),
        grid=(m // tm, n // tn),
        in_specs=[pl.BlockSpec((tm, dp), lambda i, j: (i, 0)),
                  pl.BlockSpec((d, tn), lambda i, j: (0, j))],
        out_specs=pl.BlockSpec((tm, tn), lambda i, j: (i, j)),
        compiler_params=_params("arbitrary", "arbitrary"),
        name="proj_gate",
    )(h, w)


def _lora_kernel(a_ref, w_ref, pv_ref, o_ref, *, tiles_per_group):
    grp = pl.program_id(1) // tiles_per_group
    pre = jnp.dot(a_ref[...].astype(BF16), w_ref[...], preferred_element_type=F32) + pv_ref[...]

    @pl.when(grp == 0)
    def _():
        o_ref[...] = -jnp.exp(-_softplus(-pre) - 0.5)

    @pl.when(grp == 1)
    def _():
        o_ref[...] = jax.nn.sigmoid(pre)

    @pl.when(grp == 2)
    def _():
        o_ref[...] = pre


def _lora(ps, w, pv, rows_per_batch, lp, col_block, dr):
    m = ps.shape[0]
    n = w.shape[1]
    tm = _tile(rows_per_batch, 1024, SUBLANES)
    tn = _tile(dr, 2048, LANES)
    return pl.pallas_call(
        functools.partial(_lora_kernel, tiles_per_group=dr // tn),
        out_shape=jax.ShapeDtypeStruct((m, n), F32),
        grid=(m // tm, n // tn),
        in_specs=[pl.BlockSpec((tm, lp), lambda i, j: (i, col_block)),
                  pl.BlockSpec((lp, tn), lambda i, j: (0, j)),
                  pl.BlockSpec((1, tn), lambda i, j: (0, j))],
        out_specs=pl.BlockSpec((tm, tn), lambda i, j: (i, j)),
        compiler_params=_params("arbitrary", "arbitrary"),
        name="lora",
    )(ps, w, pv)


def _split(x, n):
    pieces = []
    for _ in range(n):
        p = x.astype(BF16)
        pieces.append(p)
        x = x - p.astype(F32)
    return pieces


def _dot(a, b, dims, na=1, nb=1):
    pa = [a] if a.dtype == BF16 else _split(a, na)
    pb = [b] if b.dtype == BF16 else _split(b, nb)
    depth = max(len(pa), len(pb))
    acc = None
    for ia, xa in enumerate(pa):
        for ib, xb in enumerate(pb):
            if ia + ib >= depth:
                continue
            d = lax.dot_general(xa, xb, dims, preferred_element_type=F32)
            acc = d if acc is None else acc + d
    return acc


def _scan_kernel(r_ref, k_ref, v_ref, lw_ref, ic_ref, g_ref, kk_ref, ka_ref, rk_ref, lnw_ref, lnb_ref,
                 o_ref, st_ref, *, nb, pairs):
    cl = CHUNK
    pw = 2 * HEAD_DIM
    chains = [(b, pp) for b in range(nb) for pp in range(pairs)]
    nch = len(chains)

    @pl.when(pl.program_id(1) == 0)
    def _():
        st_ref[...] = jnp.zeros_like(st_ref)

    ri = lax.broadcasted_iota(jnp.int32, (2 * cl, 2 * cl), 0)
    ci = lax.broadcasted_iota(jnp.int32, (2 * cl, 2 * cl), 1)
    strict = ri > ci
    incl = ri >= ci
    eye = jnp.where(ri == ci, 1.0, 0.0).astype(F32)
    hi = lax.broadcasted_iota(jnp.int32, (pw, pw), 0) // HEAD_DIM
    hj = lax.broadcasted_iota(jnp.int32, (pw, pw), 1) // HEAD_DIM
    head_ones = jnp.where(hi == hj, 1.0, 0.0).astype(BF16)
    ti = lax.broadcasted_iota(jnp.int32, (cl, cl), 0)
    tj = lax.broadcasted_iota(jnp.int32, (cl, cl), 1)
    cum_ones = jnp.where(ti >= tj, 1.0, 0.0).astype(BF16)
    first_head = lax.broadcasted_iota(jnp.int32, (1, pw), 1) < HEAD_DIM
    inv_n = 1.0 / HEAD_DIM

    def block_diag(x):
        return jnp.concatenate([jnp.where(first_head, x, 0.0), jnp.where(first_head, 0.0, x)], axis=0)

    def head_sums(xs):
        s = _dot(jnp.concatenate(xs, axis=0), head_ones, _NN)
        return [s[i * cl:(i + 1) * cl] for i in range(nch)]

    def tok(ref):
        return [ref[b, :, pp * pw:(pp + 1) * pw] for b, pp in chains]

    def par(ref):
        return [ref[:, pp * pw:(pp + 1) * pw] for _, pp in chains]

    def each(f, *ls):
        return [f(*xs) for xs in zip(*ls)]

    def rows(*xs):
        return jnp.concatenate(xs, axis=0)

    def cols(*xs):
        return jnp.concatenate(xs, axis=1)

    r, kraw, v, lw, ic, g = tok(r_ref), tok(k_ref), tok(v_ref), tok(lw_ref), tok(ic_ref), tok(g_ref)

    kk = each(lambda k, p: k * p, kraw, par(kk_ref))
    ss = head_sums(each(lambda x: x * x, kk))
    kk = each(lambda x, s: x / jnp.maximum(jnp.sqrt(s), 1e-12), kk, ss)
    kmod = each(lambda k, i, p: k * (1.0 + (i - 1.0) * p), kraw, ic, par(ka_ref))
    bvec = each(lambda x, i: x * i, kk, ic)

    lg_b = [_dot(cum_ones, lw_ref[b], _NN, nb=2) for b in range(nb)]
    lg = [lg_b[b][:, pp * pw:(pp + 1) * pw] for b, pp in chains]
    gcol = each(lambda x: jnp.transpose(jnp.broadcast_to(jnp.exp(x[cl - 1:cl, :]), (pw, pw))), lg)
    g_inv = each(lambda x: jnp.exp(-x), lg)
    g_end = each(lambda x: jnp.exp(x[cl - 1:cl, :] - x), lg)

    a_bd = each(lambda x, l, w: block_diag(-x * jnp.exp(l - w)), kk, lg, lw)
    r_bd = each(lambda x, l: block_diag(x * jnp.exp(l)), r, lg)
    b_bd = each(lambda x, gi: block_diag(x * gi), bvec, g_inv)
    k_bd = each(lambda x, gi: block_diag(x * gi), kmod, g_inv)
    v_bd = each(block_diag, v)
    bend_bd = each(lambda x, ge: block_diag(x * ge), bvec, g_end)
    kend_bd = each(lambda x, ge: block_diag(x * ge), kmod, g_end)

    sc = each(lambda a, rr, b, k: _dot(jnp.concatenate([a, rr], axis=0), jnp.concatenate([b, k], axis=0), _NT),
              a_bd, r_bd, b_bd, k_bd)
    l_ab = each(lambda s: jnp.where(strict, s[:2 * cl, :2 * cl], 0.0), sc)
    m_ak = each(lambda s: jnp.where(strict, s[:2 * cl, 2 * cl:], 0.0), sc)
    m_rb = each(lambda s: jnp.where(incl, s[2 * cl:, :2 * cl], 0.0), sc)
    m_rk = each(lambda s: jnp.where(incl, s[2 * cl:, 2 * cl:], 0.0), sc)

    n_sq = cl.bit_length() - 2
    tinv = each(lambda l: eye + l, l_ab)
    lpow = each(lambda p: _dot(p, p, _NN), l_ab)
    for _ in range(n_sq - 1):
        z = each(lambda p, t: _dot(rows(p, t), p, _NN), lpow, tinv)
        lpow = each(lambda zz: zz[:2 * cl], z)
        tinv = each(lambda t, zz: t + zz[2 * cl:], tinv, z)
    tinv = each(lambda t, p: t + _dot(t, p, _NN), tinv, lpow)

    st = [st_ref[i] for i in range(nch)]
    rhs = each(lambda a, mm, s, vv: _dot(cols(a, mm), rows(s, vv), _NN), a_bd, m_ak, st, v_bd)
    u = each(lambda t, x: _dot(t, x, _NN), tinv, rhs)
    y_bd = each(lambda rr, mb, mk, s, uu, vv: _dot(cols(rr, mb, mk), rows(s, uu, vv), _NN),
                r_bd, m_rb, m_rk, st, u, v_bd)
    for i in range(nch):
        st_ref[i] = gcol[i] * st[i] + _dot(rows(bend_bd[i], kend_bd[i]), rows(u[i], v_bd[i]), _TN)
    y = each(lambda x: x[:cl] + x[cl:], y_bd)

    mean = each(lambda s: s * inv_n, head_sums(y))
    dy = each(lambda a, b: a - b, y, mean)
    var = each(lambda s: s * inv_n, head_sums(each(lambda x: x * x, dy)))
    bon = head_sums(each(lambda a, b, p: a * b * p, r, kmod, par(rk_ref)))
    for i, (b, pp) in enumerate(chains):
        sl = slice(pp * pw, (pp + 1) * pw)
        yn = dy[i] * lax.rsqrt(var[i] + LNX_EPS) * lnw_ref[:, sl] + lnb_ref[:, sl]
        o_ref[b, :, sl] = ((yn + bon[i] * v[i]) * g[i]).astype(o_ref.dtype)


def _scan(ps, lig, k_k, k_a, r_k, lnx_w, lnx_b, nb, seq, dr):
    pw = 2 * HEAD_DIM
    pairs = 4 if dr % (4 * pw) == 0 else 1
    bw = pairs * pw
    npb = dr // bw
    nc = seq // CHUNK
    ps3 = ps.reshape(nb, seq, ps.shape[1])
    lig3 = lig.reshape(nb, seq, lig.shape[1])
    tok = lambda off: pl.BlockSpec((nb, CHUNK, bw), lambda p, c: (0, c, off * npb + p))
    par = pl.BlockSpec((1, bw), lambda p, c: (0, p))
    o = pl.pallas_call(
        functools.partial(_scan_kernel, nb=nb, pairs=pairs),
        out_shape=jax.ShapeDtypeStruct((nb, seq, dr), BF16),
        grid=(npb, nc),
        in_specs=[tok(0), tok(1), tok(2), tok(0), tok(1), tok(2), par, par, par, par, par],
        out_specs=pl.BlockSpec((nb, CHUNK, bw), lambda p, c: (0, c, p)),
        scratch_shapes=[pltpu.VMEM((nb * pairs, pw, pw), F32)],
        compiler_params=_params("arbitrary", "arbitrary"),
        name="rwkv_scan",
    )(ps3, ps3, ps3, lig3, lig3, lig3, k_k, k_a, r_k, lnx_w, lnx_b)
    return o.reshape(nb * seq, dr)


def _merge_kernel(o_ref, y_ref, wa_ref, wb_ref, sa_ref, sb_ref, out_ref):
    ya = jnp.dot(o_ref[...], wa_ref[...], preferred_element_type=F32)
    yb = jnp.dot(y_ref[...], wb_ref[...], preferred_element_type=F32)
    out_ref[...] = (sa_ref[...].astype(F32) * ya + sb_ref[...].astype(F32) * yb).astype(out_ref.dtype)


def _merge(o, ycb, wa, wb, sg, rows_per_batch):
    m, dr = o.shape
    dc = ycb.shape[1]
    d = wa.shape[1]
    tm = _tile(rows_per_batch, 1024, SUBLANES)
    tn = _tile(d, 512, LANES)
    nj = d // tn
    return pl.pallas_call(
        _merge_kernel,
        out_shape=jax.ShapeDtypeStruct((m, d), BF16),
        grid=(m // tm, nj),
        in_specs=[pl.BlockSpec((tm, dr), lambda i, j: (i, 0)),
                  pl.BlockSpec((tm, dc), lambda i, j: (i, 0)),
                  pl.BlockSpec((dr, tn), lambda i, j: (0, j)),
                  pl.BlockSpec((dc, tn), lambda i, j: (0, j)),
                  pl.BlockSpec((tm, tn), lambda i, j: (i, j)),
                  pl.BlockSpec((tm, tn), lambda i, j: (i, nj + j))],
        out_specs=pl.BlockSpec((tm, tn), lambda i, j: (i, j)),
        compiler_params=_params("arbitrary", "arbitrary"),
        name="merge",
    )(o, ycb, wa, wb, sg, sg)


def _resid_kernel(a_ref, w_ref, x_ref, gt_ref, o_ref):
    y = jnp.dot(a_ref[...], w_ref[...], preferred_element_type=F32)
    o_ref[...] = x_ref[...] + gt_ref[...] * y


def _resid(a, w, x2d, mod3, gate_idx, rows_per_batch, tm_pref, name, cols_outer):
    m, kd = a.shape
    d = w.shape[1]
    tm = _tile(rows_per_batch, tm_pref, SUBLANES)
    tn = _tile(d, 512, LANES)
    tpb = rows_per_batch // tm
    if cols_outer:
        grid = (d // tn, m // tm)
        ij = lambda f: (lambda j, i: f(i, j))
    else:
        grid = (m // tm, d // tn)
        ij = lambda f: f
    return pl.pallas_call(
        _resid_kernel,
        out_shape=jax.ShapeDtypeStruct((m, d), F32),
        grid=grid,
        in_specs=[pl.BlockSpec((tm, kd), ij(lambda i, j: (i, 0))),
                  pl.BlockSpec((kd, tn), ij(lambda i, j: (0, j))),
                  pl.BlockSpec((tm, tn), ij(lambda i, j: (i, j))),
                  pl.BlockSpec((None, 1, tn), ij(lambda i, j: ((i // tpb) * N_ADA + gate_idx, 0, j)))],
        out_specs=pl.BlockSpec((tm, tn), ij(lambda i, j: (i, j))),
        compiler_params=_params("arbitrary", "arbitrary"),
        name=name,
    )(a, w, x2d, mod3)


def _ffn_up_kernel(h_ref, wg_ref, wv_ref, cw_ref, o_ref, carry_ref, *, tpb, n_sub):
    i, j = pl.program_id(0), pl.program_id(1)
    ts = o_ref.shape[0] // n_sub
    cw = cw_ref[...]

    def finish(s, gate, val, prev8):
        gc = _causal_conv3(gate, prev8, cw)
        o_ref[s * ts:(s + 1) * ts, :] = (gc * jax.nn.sigmoid(gc) * val).astype(o_ref.dtype)

    prev8 = jnp.where((i % tpb) == 0, 0.0, carry_ref[j])
    pending = None
    for s in range(n_sub):
        h = h_ref[s * ts:(s + 1) * ts, :wg_ref.shape[0]]
        gate = jnp.dot(h, wg_ref[...], preferred_element_type=F32)
        val = jnp.dot(h, wv_ref[...], preferred_element_type=F32)
        if pending is not None:
            finish(*pending)
        pending = (s, gate, val, prev8)
        prev8 = gate[ts - SUBLANES:]
    finish(*pending)
    carry_ref[j] = prev8


def _ffn_up(h, w, cw, rows_per_batch):
    m, dp = h.shape
    d = w.shape[0]
    dff = cw.shape[1]
    tm = _tile(rows_per_batch, 1024, SUBLANES)
    tc = _tile(dff, 256, LANES)
    ncb = dff // tc
    return pl.pallas_call(
        functools.partial(_ffn_up_kernel, tpb=rows_per_batch // tm, n_sub=8 if tm % 64 == 0 else 1),
        out_shape=jax.ShapeDtypeStruct((m, dff), BF16),
        grid=(m // tm, ncb),
        in_specs=[pl.BlockSpec((tm, dp), lambda i, j: (i, 0)),
                  pl.BlockSpec((d, tc), lambda i, j: (0, j)),
                  pl.BlockSpec((d, tc), lambda i, j: (0, ncb + j)),
                  pl.BlockSpec((CONV_WIDTH, tc), lambda i, j: (0, j))],
        out_specs=pl.BlockSpec((tm, tc), lambda i, j: (i, j)),
        scratch_shapes=[pltpu.VMEM((ncb, SUBLANES, tc), F32)],
        compiler_params=_params("arbitrary", "arbitrary"),
        name="ffn_up",
    )(h, w, w, cw)


def kernel(x, c, w_ada, b_ada, norm1_gain, w_in, mu_shift, w0, a0, k_k, k_a, r_k, w_lora_decay,
           w_lora_iclr, w_lora_gate, lnx_w, lnx_b, conv_w_mix, w_o_rwkv, w_o_conv, w_out, norm2_gain,
           w_ffn_up, conv_w_ffn, w_ffn_down, final_gain):
    nb, seq, d = x.shape
    m = nb * seq
    depth = w_ada.shape[0]
    dr = w_o_rwkv.shape[1]
    dc = w_o_conv.shape[1]
    n_dec, n_icl, n_gat = w_lora_decay.shape[1], w_lora_iclr.shape[1], w_lora_gate.shape[1]
    n_lora = n_dec + n_icl + n_gat
    n_shift = 3 * dr + n_lora
    assert seq % CHUNK == 0 and dr % (2 * HEAD_DIM) == 0

    tn_a = _tile(dr, 512, LANES)
    lp = _round_up(n_lora, tn_a)
    assert (3 * dr) % lp == 0

    x2d = x.reshape(m, d)
    for layer in range(depth):
        wi = w_in[layer]
        w_a = jnp.concatenate(
            [wi[:, :3 * dr], jnp.pad(wi[:, 3 * dr:n_shift], ((0, 0), (0, lp - n_lora)))], axis=1).astype(BF16)
        mu_a = jnp.concatenate([mu_shift[layer][:3 * dr], jnp.pad(mu_shift[layer][3 * dr:], (0, lp - n_lora))])
        w_c = wi[:, n_shift:n_shift + 3 * dc].astype(BF16)
        w_g = wi[:, n_shift + 3 * dc:].astype(BF16)
        w_l = jnp.zeros((lp, 3 * dr), F32)
        w_l = w_l.at[:n_dec, :dr].set(w_lora_decay[layer])
        w_l = w_l.at[n_dec:n_dec + n_icl, dr:2 * dr].set(w_lora_iclr[layer])
        w_l = w_l.at[n_dec + n_icl:n_lora, 2 * dr:].set(w_lora_gate[layer]).astype(BF16)
        pv_l = jnp.concatenate([w0[layer], a0[layer], jnp.zeros((dr,), F32)]).reshape(1, 3 * dr)
        row = lambda p: p.reshape(1, dr)

        mod3 = _ada(c, w_ada[layer], b_ada[layer]).reshape(nb * N_ADA, 1, d)
        h = _norm_mod(x2d, norm1_gain[layer], mod3, 0, 1, seq)
        ps = _proj_shift(h, w_a, mu_a.reshape(1, -1), seq, 3 * dr // tn_a, tn_a, n_dec, n_icl)
        ycb = _proj_conv(h, w_c, conv_w_mix[layer], seq)
        sg = _proj_gate(h, w_g, seq)
        lig = _lora(ps, w_l, pv_l, seq, lp, 3 * dr // lp, dr)
        o = _scan(ps, lig, row(k_k[layer]), row(k_a[layer]), row(r_k[layer]), row(lnx_w[layer]),
                  row(lnx_b[layer]), nb, seq, dr)
        merged = _merge(o, ycb, w_o_rwkv[layer].astype(BF16), w_o_conv[layer].astype(BF16), sg, seq)
        x1 = _resid(merged, w_out[layer].astype(BF16), x2d, mod3, 2, seq, 1024, "attn_out", False)
        h2 = _norm_mod(x1, norm2_gain[layer], mod3, 3, 4, seq)
        act = _ffn_up(h2, w_ffn_up[layer].astype(BF16), conv_w_ffn[layer], seq)
        x2d = _resid(act, w_ffn_down[layer].astype(BF16), x1, mod3, 5, seq, 512, "ffn_down", True)
    return _final_norm(x2d, final_gain).reshape(nb, seq, d)
```

```python
import functools

import jax
import jax.numpy as jnp
from jax import lax
from jax.experimental import pallas as pl
from jax.experimental.pallas import tpu as pltpu

F32 = jnp.float32
BF16 = jnp.bfloat16

NORM_EPS = 1e-6
LNX_EPS = 64e-5
HEAD_DIM = 64
N_ADA = 6
CONV_WIDTH = 3
LANES = 128
SUBLANES = 8
CHUNK = 64
VMEM_LIMIT = 56 * 1024 * 1024
LHS_PAD = LANES

_NN = (((1,), (0,)), ((), ()))
_NT = (((1,), (1,)), ((), ()))
_TN = (((0,), (0,)), ((), ()))


def _round_up(n, m):
    return (n + m - 1) // m * m


def _tile(n, pref, quantum):
    t = min(pref, n) // quantum * quantum
    while t > quantum and n % t:
        t -= quantum
    assert t > 0 and n % t == 0, (n, pref, quantum)
    return t


def _params(*sem):
    return pltpu.CompilerParams(dimension_semantics=sem, vmem_limit_bytes=VMEM_LIMIT)


def _ada_kernel(cb_ref, w_ref, b_ref, o_ref, *, nb, kd, tn):
    rep = tn // LANES

    def body(kc, accs):
        k0 = pl.multiple_of(kc * SUBLANES, SUBLANES)
        w = w_ref[pl.ds(k0, SUBLANES), :]
        new = []
        for b in range(nb):
            cv = cb_ref[b, pl.ds(k0, SUBLANES), :]
            cv = cv * jax.nn.sigmoid(cv)
            new.append(accs[b] + w * jnp.concatenate([cv] * rep, axis=1))
        return tuple(new)

    init = tuple(jnp.zeros((SUBLANES, tn), F32) for _ in range(nb))
    accs = lax.fori_loop(0, kd // SUBLANES, body, init, unroll=8)
    rows = [jnp.sum(a, axis=0, keepdims=True) for a in accs]
    o_ref[...] = jnp.concatenate(rows, axis=0) + b_ref[...]


def _ada(c, w, b):
    nb, kd = c.shape
    n = w.shape[1]
    tn = _tile(n, 512, LANES)
    cb = jnp.broadcast_to(c[:, :, None], (nb, kd, LANES))
    return pl.pallas_call(
        functools.partial(_ada_kernel, nb=nb, kd=kd, tn=tn),
        out_shape=jax.ShapeDtypeStruct((nb, n), F32),
        grid=(n // tn,),
        in_specs=[pl.BlockSpec((nb, kd, LANES), lambda j: (0, 0, 0)),
                  pl.BlockSpec((kd, tn), lambda j: (0, j)),
                  pl.BlockSpec((1, tn), lambda j: (0, j))],
        out_specs=pl.BlockSpec((nb, tn), lambda j: (0, j)),
        compiler_params=_params("arbitrary"),
        name="ada",
    )(cb, w, b.reshape(1, n))


def _rms(x):
    return x * lax.rsqrt(jnp.mean(x * x, axis=-1, keepdims=True) + NORM_EPS)


def _norm_mod_kernel(x_ref, g_ref, sh_ref, sc_ref, o_ref):
    y = _rms(x_ref[...]) * g_ref[...]
    d = x_ref.shape[1]
    o_ref[:, :d] = (y * (1.0 + sc_ref[...]) + sh_ref[...]).astype(o_ref.dtype)
    o_ref[:, d:] = jnp.zeros((o_ref.shape[0], o_ref.shape[1] - d), o_ref.dtype)


def _norm_mod(x2d, gain, mod3, shift_idx, scale_idx, rows_per_batch):
    m, d = x2d.shape
    tm = _tile(rows_per_batch, 256, SUBLANES)
    tpb = rows_per_batch // tm
    return pl.pallas_call(
        _norm_mod_kernel,
        out_shape=jax.ShapeDtypeStruct((m, d + LHS_PAD), BF16),
        grid=(m // tm,),
        in_specs=[pl.BlockSpec((tm, d), lambda i: (i, 0)),
                  pl.BlockSpec((1, d), lambda i: (0, 0)),
                  pl.BlockSpec((None, 1, d), lambda i: ((i // tpb) * N_ADA + shift_idx, 0, 0)),
                  pl.BlockSpec((None, 1, d), lambda i: ((i // tpb) * N_ADA + scale_idx, 0, 0))],
        out_specs=pl.BlockSpec((tm, d + LHS_PAD), lambda i: (i, 0)),
        compiler_params=_params("arbitrary"),
        name="norm_mod",
    )(x2d, gain.reshape(1, d), mod3, mod3)


def _final_norm_kernel(x_ref, g_ref, o_ref):
    o_ref[...] = _rms(x_ref[...]) * g_ref[...]


def _final_norm(x2d, gain):
    m, d = x2d.shape
    tm = _tile(m, 256, SUBLANES)
    return pl.pallas_call(
        _final_norm_kernel,
        out_shape=jax.ShapeDtypeStruct((m, d), F32),
        grid=(m // tm,),
        in_specs=[pl.BlockSpec((tm, d), lambda i: (i, 0)),
                  pl.BlockSpec((1, d), lambda i: (0, 0))],
        out_specs=pl.BlockSpec((tm, d), lambda i: (i, 0)),
        compiler_params=_params("arbitrary"),
        name="final_norm",
    )(x2d, gain.reshape(1, d))


def _shift_rows(p, prev8, s):
    rolled = pltpu.roll(p, s, 0)
    row = lax.broadcasted_iota(jnp.int32, prev8.shape, 0)
    head = jnp.where(row < s, pltpu.roll(prev8, s, 0), rolled[:SUBLANES])
    return jnp.concatenate([head, rolled[SUBLANES:]], axis=0)


def _carry_swap(carry_ref, j, tail, first):
    prev8 = jnp.where(first, 0.0, carry_ref[j])
    carry_ref[j] = tail
    return prev8


def _causal_conv3(z, prev8, cw):
    return cw[0:1] * _shift_rows(z, prev8, 2) + cw[1:2] * _shift_rows(z, prev8, 1) + cw[2:3] * z


def _softplus(x):
    return jnp.maximum(x, 0.0) + jnp.log(1.0 + jnp.exp(-jnp.abs(x)))


def _proj_shift_kernel(h_ref, w_ref, mu_ref, o_ref, carry_ref, *, tpb, j_lora, n_tanh, n_lin):
    i, j = pl.program_id(0), pl.program_id(1)
    tm, tn = o_ref.shape
    p = jnp.dot(h_ref[:, :w_ref.shape[0]], w_ref[...], preferred_element_type=F32)
    prev8 = _carry_swap(carry_ref, j, p[tm - SUBLANES:], (i % tpb) == 0)
    out = p + (_shift_rows(p, prev8, 1) - p) * mu_ref[...]

    @pl.when(j < j_lora)
    def _():
        o_ref[...] = out

    @pl.when(j >= j_lora)
    def _():
        col = (j - j_lora) * tn + lax.broadcasted_iota(jnp.int32, (1, tn), 1)
        act = jnp.where(col < n_tanh, jnp.tanh(out),
                        jnp.where(col < n_tanh + n_lin, out, jax.nn.sigmoid(out)))
        o_ref[...] = act


def _proj_shift(h, w, mu, rows_per_batch, j_lora, tn, n_tanh, n_lin):
    m, dp = h.shape
    d = w.shape[0]
    n = w.shape[1]
    tm = _tile(rows_per_batch, 1024, SUBLANES)
    return pl.pallas_call(
        functools.partial(_proj_shift_kernel, tpb=rows_per_batch // tm, j_lora=j_lora,
                          n_tanh=n_tanh, n_lin=n_lin),
        out_shape=jax.ShapeDtypeStruct((m, n), F32),
        grid=(m // tm, n // tn),
        in_specs=[pl.BlockSpec((tm, dp), lambda i, j: (i, 0)),
                  pl.BlockSpec((d, tn), lambda i, j: (0, j)),
                  pl.BlockSpec((1, tn), lambda i, j: (0, j))],
        out_specs=pl.BlockSpec((tm, tn), lambda i, j: (i, j)),
        scratch_shapes=[pltpu.VMEM((n // tn, SUBLANES, tn), F32)],
        compiler_params=_params("arbitrary", "arbitrary"),
        name="proj_shift",
    )(h, w, mu)


def _proj_conv_kernel(h_ref, wb_ref, wc_ref, wx_ref, cw_ref, o_ref, carry_ref, *, tpb):
    i, j = pl.program_id(0), pl.program_id(1)
    tm = o_ref.shape[0]
    h = h_ref[:, :wb_ref.shape[0]]
    z = (jnp.dot(h, wc_ref[...], preferred_element_type=F32)
         * jnp.dot(h, wx_ref[...], preferred_element_type=F32))
    prev8 = _carry_swap(carry_ref, j, z[tm - SUBLANES:], (i % tpb) == 0)
    conv = _causal_conv3(z, prev8, cw_ref[...])
    o_ref[...] = (jnp.dot(h, wb_ref[...], preferred_element_type=F32) * conv).astype(o_ref.dtype)


def _proj_conv(h, w, cw, rows_per_batch):
    m, dp = h.shape
    d = w.shape[0]
    dc = cw.shape[1]
    tm = _tile(rows_per_batch, 1024, SUBLANES)
    tc = _tile(dc, 512, LANES)
    ncb = dc // tc
    wspec = lambda grp: pl.BlockSpec((d, tc), lambda i, j: (0, grp * ncb + j))
    return pl.pallas_call(
        functools.partial(_proj_conv_kernel, tpb=rows_per_batch // tm),
        out_shape=jax.ShapeDtypeStruct((m, dc), BF16),
        grid=(m // tm, ncb),
        in_specs=[pl.BlockSpec((tm, dp), lambda i, j: (i, 0)), wspec(0), wspec(1), wspec(2),
                  pl.BlockSpec((CONV_WIDTH, tc), lambda i, j: (0, j))],
        out_specs=pl.BlockSpec((tm, tc), lambda i, j: (i, j)),
        scratch_shapes=[pltpu.VMEM((ncb, SUBLANES, tc), F32)],
        compiler_params=_params("arbitrary", "arbitrary"),
        name="proj_conv",
    )(h, w, w, w, cw)


def _proj_gate_kernel(h_ref, w_ref, o_ref):
    p = jnp.dot(h_ref[:, :w_ref.shape[0]], w_ref[...], preferred_element_type=F32)
    o_ref[...] = jax.nn.sigmoid(p).astype(o_ref.dtype)


def _proj_gate(h, w, rows_per_batch):
    m, dp = h.shape
    d = w.shape[0]
    n = w.shape[1]
    tm = _tile(rows_per_batch, 1024, SUBLANES)
    tn = _tile(n, 1024, LANES)
    return pl.pallas_call(
        _proj_gate_kernel,
        out_shape=jax.ShapeDtypeStruct((m, n), BF16),
        grid=(m // tm, n // tn),
        in_specs=[pl.BlockSpec((tm, dp), lambda i, j: (i, 0)),
                  pl.BlockSpec((d, tn), lambda i, j: (0, j))],
        out_specs=pl.BlockSpec((tm, tn), lambda i, j: (i, j)),
        compiler_params=_params("arbitrary", "arbitrary"),
        name="proj_gate",
    )(h, w)


def _lora_kernel(a_ref, w_ref, pv_ref, o_ref, *, tiles_per_group):
    grp = pl.program_id(1) // tiles_per_group
    pre = jnp.dot(a_ref[...].astype(BF16), w_ref[...], preferred_element_type=F32) + pv_ref[...]

    @pl.when(grp == 0)
    def _():
        o_ref[...] = -jnp.exp(-_softplus(-pre) - 0.5)

    @pl.when(grp == 1)
    def _():
        o_ref[...] = jax.nn.sigmoid(pre)

    @pl.when(grp == 2)
    def _():
        o_ref[...] = pre


def _lora(ps, w, pv, rows_per_batch, lp, col_block, dr):
    m = ps.shape[0]
    n = w.shape[1]
    tm = _tile(rows_per_batch, 1024, SUBLANES)
    tn = _tile(dr, 2048, LANES)
    return pl.pallas_call(
        functools.partial(_lora_kernel, tiles_per_group=dr // tn),
        out_shape=jax.ShapeDtypeStruct((m, n), F32),
        grid=(m // tm, n // tn),
        in_specs=[pl.BlockSpec((tm, lp), lambda i, j: (i, col_block)),
                  pl.BlockSpec((lp, tn), lambda i, j: (0, j)),
                  pl.BlockSpec((1, tn), lambda i, j: (0, j))],
        out_specs=pl.BlockSpec((tm, tn), lambda i, j: (i, j)),
        compiler_params=_params("arbitrary", "arbitrary"),
        name="lora",
    )(ps, w, pv)


def _split(x, n):
    pieces = []
    for _ in range(n):
        p = x.astype(BF16)
        pieces.append(p)
        x = x - p.astype(F32)
    return pieces


def _dot(a, b, dims, na=1, nb=1):
    pa = [a] if a.dtype == BF16 else _split(a, na)
    pb = [b] if b.dtype == BF16 else _split(b, nb)
    depth = max(len(pa), len(pb))
    acc = None
    for ia, xa in enumerate(pa):
        for ib, xb in enumerate(pb):
            if ia + ib >= depth:
                continue
            d = lax.dot_general(xa, xb, dims, preferred_element_type=F32)
            acc = d if acc is None else acc + d
    return acc


def _scan_kernel(r_ref, k_ref, v_ref, lw_ref, ic_ref, g_ref, kk_ref, ka_ref, rk_ref, lnw_ref, lnb_ref,
                 o_ref, st_ref, *, nb, pairs):
    cl = CHUNK
    pw = 2 * HEAD_DIM
    chains = [(b, pp) for b in range(nb) for pp in range(pairs)]
    nch = len(chains)

    @pl.when(pl.program_id(1) == 0)
    def _():
        st_ref[...] = jnp.zeros_like(st_ref)

    ri = lax.broadcasted_iota(jnp.int32, (2 * cl, 2 * cl), 0)
    ci = lax.broadcasted_iota(jnp.int32, (2 * cl, 2 * cl), 1)
    strict = ri > ci
    incl = ri >= ci
    eye = jnp.where(ri == ci, 1.0, 0.0).astype(F32)
    hi = lax.broadcasted_iota(jnp.int32, (pw, pw), 0) // HEAD_DIM
    hj = lax.broadcasted_iota(jnp.int32, (pw, pw), 1) // HEAD_DIM
    head_ones = jnp.where(hi == hj, 1.0, 0.0).astype(BF16)
    ti = lax.broadcasted_iota(jnp.int32, (cl, cl), 0)
    tj = lax.broadcasted_iota(jnp.int32, (cl, cl), 1)
    cum_ones = jnp.where(ti >= tj, 1.0, 0.0).astype(BF16)
    first_head = lax.broadcasted_iota(jnp.int32, (1, pw), 1) < HEAD_DIM
    inv_n = 1.0 / HEAD_DIM

    def block_diag(x):
        return jnp.concatenate([jnp.where(first_head, x, 0.0), jnp.where(first_head, 0.0, x)], axis=0)

    def head_sums(xs):
        s = _dot(jnp.concatenate(xs, axis=0), head_ones, _NN)
        return [s[i * cl:(i + 1) * cl] for i in range(nch)]

    def tok(ref):
        return [ref[b, :, pp * pw:(pp + 1) * pw] for b, pp in chains]

    def par(ref):
        return [ref[:, pp * pw:(pp + 1) * pw] for _, pp in chains]

    def each(f, *ls):
        return [f(*xs) for xs in zip(*ls)]

    def rows(*xs):
        return jnp.concatenate(xs, axis=0)

    def cols(*xs):
        return jnp.concatenate(xs, axis=1)

    r, kraw, v, lw, ic, g = tok(r_ref), tok(k_ref), tok(v_ref), tok(lw_ref), tok(ic_ref), tok(g_ref)

    kk = each(lambda k, p: k * p, kraw, par(kk_ref))
    ss = head_sums(each(lambda x: x * x, kk))
    kk = each(lambda x, s: x / jnp.maximum(jnp.sqrt(s), 1e-12), kk, ss)
    kmod = each(lambda k, i, p: k * (1.0 + (i - 1.0) * p), kraw, ic, par(ka_ref))
    bvec = each(lambda x, i: x * i, kk, ic)

    lg_b = [_dot(cum_ones, lw_ref[b], _NN, nb=2) for b in range(nb)]
    lg = [lg_b[b][:, pp * pw:(pp + 1) * pw] for b, pp in chains]
    gcol = each(lambda x: jnp.transpose(jnp.broadcast_to(jnp.exp(x[cl - 1:cl, :]), (pw, pw))), lg)
    g_inv = each(lambda x: jnp.exp(-x), lg)
    g_end = each(lambda x: jnp.exp(x[cl - 1:cl, :] - x), lg)

    a_bd = each(lambda x, l, w: block_diag(-x * jnp.exp(l - w)), kk, lg, lw)
    r_bd = each(lambda x, l: block_diag(x * jnp.exp(l)), r, lg)
    b_bd = each(lambda x, gi: block_diag(x * gi), bvec, g_inv)
    k_bd = each(lambda x, gi: block_diag(x * gi), kmod, g_inv)
    v_bd = each(block_diag, v)
    bend_bd = each(lambda x, ge: block_diag(x * ge), bvec, g_end)
    kend_bd = each(lambda x, ge: block_diag(x * ge), kmod, g_end)

    sc = each(lambda a, rr, b, k: _dot(jnp.concatenate([a, rr], axis=0), jnp.concatenate([b, k], axis=0), _NT),
              a_bd, r_bd, b_bd, k_bd)
    l_ab = each(lambda s: jnp.where(strict, s[:2 * cl, :2 * cl], 0.0), sc)
    m_ak = each(lambda s: jnp.where(strict, s[:2 * cl, 2 * cl:], 0.0), sc)
    m_rb = each(lambda s: jnp.where(incl, s[2 * cl:, :2 * cl], 0.0), sc)
    m_rk = each(lambda s: jnp.where(incl, s[2 * cl:, 2 * cl:], 0.0), sc)

    n_sq = cl.bit_length() - 2
    tinv = each(lambda l: eye + l, l_ab)
    lpow = each(lambda p: _dot(p, p, _NN), l_ab)
    for _ in range(n_sq - 1):
        z = each(lambda p, t: _dot(rows(p, t), p, _NN), lpow, tinv)
        lpow = each(lambda zz: zz[:2 * cl], z)
        tinv = each(lambda t, zz: t + zz[2 * cl:], tinv, z)
    tinv = each(lambda t, p: t + _dot(t, p, _NN), tinv, lpow)

    st = [st_ref[i] for i in range(nch)]
    rhs = each(lambda a, mm, s, vv: _dot(cols(a, mm), rows(s, vv), _NN), a_bd, m_ak, st, v_bd)
    u = each(lambda t, x: _dot(t, x, _NN), tinv, rhs)
    y_bd = each(lambda rr, mb, mk, s, uu, vv: _dot(cols(rr, mb, mk), rows(s, uu, vv), _NN),
                r_bd, m_rb, m_rk, st, u, v_bd)
    for i in range(nch):
        st_ref[i] = gcol[i] * st[i] + _dot(rows(bend_bd[i], kend_bd[i]), rows(u[i], v_bd[i]), _TN)
    y = each(lambda x: x[:cl] + x[cl:], y_bd)

    mean = each(lambda s: s * inv_n, head_sums(y))
    dy = each(lambda a, b: a - b, y, mean)
    var = each(lambda s: s * inv_n, head_sums(each(lambda x: x * x, dy)))
    bon = head_sums(each(lambda a, b, p: a * b * p, r, kmod, par(rk_ref)))
    for i, (b, pp) in enumerate(chains):
        sl = slice(pp * pw, (pp + 1) * pw)
        yn = dy[i] * lax.rsqrt(var[i] + LNX_EPS) * lnw_ref[:, sl] + lnb_ref[:, sl]
        o_ref[b, :, sl] = ((yn + bon[i] * v[i]) * g[i]).astype(o_ref.dtype)


def _scan(ps, lig, k_k, k_a, r_k, lnx_w, lnx_b, nb, seq, dr):
    pw = 2 * HEAD_DIM
    pairs = 8 if dr % (8 * pw) == 0 else 1
    bw = pairs * pw
    npb = dr // bw
    nc = seq // CHUNK
    ps3 = ps.reshape(nb, seq, ps.shape[1])
    lig3 = lig.reshape(nb, seq, lig.shape[1])
    tok = lambda off: pl.BlockSpec((nb, CHUNK, bw), lambda p, c: (0, c, off * npb + p))
    par = pl.BlockSpec((1, bw), lambda p, c: (0, p))
    o = pl.pallas_call(
        functools.partial(_scan_kernel, nb=nb, pairs=pairs),
        out_shape=jax.ShapeDtypeStruct((nb, seq, dr), BF16),
        grid=(npb, nc),
        in_specs=[tok(0), tok(1), tok(2), tok(0), tok(1), tok(2), par, par, par, par, par],
        out_specs=pl.BlockSpec((nb, CHUNK, bw), lambda p, c: (0, c, p)),
        scratch_shapes=[pltpu.VMEM((nb * pairs, pw, pw), F32)],
        compiler_params=_params("arbitrary", "arbitrary"),
        name="rwkv_scan",
    )(ps3, ps3, ps3, lig3, lig3, lig3, k_k, k_a, r_k, lnx_w, lnx_b)
    return o.reshape(nb * seq, dr)


def _merge_kernel(o_ref, y_ref, wa_ref, wb_ref, sa_ref, sb_ref, out_ref):
    ya = jnp.dot(o_ref[...], wa_ref[...], preferred_element_type=F32)
    yb = jnp.dot(y_ref[...], wb_ref[...], preferred_element_type=F32)
    out_ref[...] = (sa_ref[...].astype(F32) * ya + sb_ref[...].astype(F32) * yb).astype(out_ref.dtype)


def _merge(o, ycb, wa, wb, sg, rows_per_batch):
    m, dr = o.shape
    dc = ycb.shape[1]
    d = wa.shape[1]
    tm = _tile(rows_per_batch, 1024, SUBLANES)
    tn = _tile(d, 1024, LANES)
    nj = d // tn
    return pl.pallas_call(
        _merge_kernel,
        out_shape=jax.ShapeDtypeStruct((m, d), BF16),
        grid=(m // tm, nj),
        in_specs=[pl.BlockSpec((tm, dr), lambda i, j: (i, 0)),
                  pl.BlockSpec((tm, dc), lambda i, j: (i, 0)),
                  pl.BlockSpec((dr, tn), lambda i, j: (0, j)),
                  pl.BlockSpec((dc, tn), lambda i, j: (0, j)),
                  pl.BlockSpec((tm, tn), lambda i, j: (i, j)),
                  pl.BlockSpec((tm, tn), lambda i, j: (i, nj + j))],
        out_specs=pl.BlockSpec((tm, tn), lambda i, j: (i, j)),
        compiler_params=_params("arbitrary", "arbitrary"),
        name="merge",
    )(o, ycb, wa, wb, sg, sg)


def _resid_kernel(a_ref, w_ref, x_ref, gt_ref, o_ref):
    y = jnp.dot(a_ref[...], w_ref[...], preferred_element_type=F32)
    o_ref[...] = x_ref[...] + gt_ref[...] * y


def _resid(a, w, x2d, mod3, gate_idx, rows_per_batch, tm_pref, tn_pref, name, cols_outer):
    m, kd = a.shape
    d = w.shape[1]
    tm = _tile(rows_per_batch, tm_pref, SUBLANES)
    tn = _tile(d, tn_pref, LANES)
    tpb = rows_per_batch // tm
    if cols_outer:
        grid = (d // tn, m // tm)
        ij = lambda f: (lambda j, i: f(i, j))
    else:
        grid = (m // tm, d // tn)
        ij = lambda f: f
    return pl.pallas_call(
        _resid_kernel,
        out_shape=jax.ShapeDtypeStruct((m, d), F32),
        grid=grid,
        in_specs=[pl.BlockSpec((tm, kd), ij(lambda i, j: (i, 0))),
                  pl.BlockSpec((kd, tn), ij(lambda i, j: (0, j))),
                  pl.BlockSpec((tm, tn), ij(lambda i, j: (i, j))),
                  pl.BlockSpec((None, 1, tn), ij(lambda i, j: ((i // tpb) * N_ADA + gate_idx, 0, j)))],
        out_specs=pl.BlockSpec((tm, tn), ij(lambda i, j: (i, j))),
        compiler_params=_params("arbitrary", "arbitrary"),
        name=name,
    )(a, w, x2d, mod3)


def _ffn_up_kernel(h_ref, wg_ref, wv_ref, cw_ref, o_ref, carry_ref, *, tpb, n_sub):
    i, j = pl.program_id(0), pl.program_id(1)
    ts = o_ref.shape[0] // n_sub
    cw = cw_ref[...]

    def finish(s, gate, val, prev8):
        gc = _causal_conv3(gate, prev8, cw)
        o_ref[s * ts:(s + 1) * ts, :] = (gc * jax.nn.sigmoid(gc) * val).astype(o_ref.dtype)

    prev8 = jnp.where((i % tpb) == 0, 0.0, carry_ref[j])
    pending = None
    for s in range(n_sub):
        h = h_ref[s * ts:(s + 1) * ts, :wg_ref.shape[0]]
        gate = jnp.dot(h, wg_ref[...], preferred_element_type=F32)
        val = jnp.dot(h, wv_ref[...], preferred_element_type=F32)
        if pending is not None:
            finish(*pending)
        pending = (s, gate, val, prev8)
        prev8 = gate[ts - SUBLANES:]
    finish(*pending)
    carry_ref[j] = prev8


def _ffn_up(h, w, cw, rows_per_batch):
    m, dp = h.shape
    d = w.shape[0]
    dff = cw.shape[1]
    tm = _tile(rows_per_batch, 2048, SUBLANES)
    tc = _tile(dff, 256, LANES)
    ncb = dff // tc
    return pl.pallas_call(
        functools.partial(_ffn_up_kernel, tpb=rows_per_batch // tm, n_sub=tm // _tile(tm, 128, SUBLANES)),
        out_shape=jax.ShapeDtypeStruct((m, dff), BF16),
        grid=(m // tm, ncb),
        in_specs=[pl.BlockSpec((tm, dp), lambda i, j: (i, 0)),
                  pl.BlockSpec((d, tc), lambda i, j: (0, j)),
                  pl.BlockSpec((d, tc), lambda i, j: (0, ncb + j)),
                  pl.BlockSpec((CONV_WIDTH, tc), lambda i, j: (0, j))],
        out_specs=pl.BlockSpec((tm, tc), lambda i, j: (i, j)),
        scratch_shapes=[pltpu.VMEM((ncb, SUBLANES, tc), F32)],
        compiler_params=_params("arbitrary", "arbitrary"),
        name="ffn_up",
    )(h, w, w, cw)


def kernel(x, c, w_ada, b_ada, norm1_gain, w_in, mu_shift, w0, a0, k_k, k_a, r_k, w_lora_decay,
           w_lora_iclr, w_lora_gate, lnx_w, lnx_b, conv_w_mix, w_o_rwkv, w_o_conv, w_out, norm2_gain,
           w_ffn_up, conv_w_ffn, w_ffn_down, final_gain):
    nb, seq, d = x.shape
    m = nb * seq
    depth = w_ada.shape[0]
    dr = w_o_rwkv.shape[1]
    dc = w_o_conv.shape[1]
    n_dec, n_icl, n_gat = w_lora_decay.shape[1], w_lora_iclr.shape[1], w_lora_gate.shape[1]
    n_lora = n_dec + n_icl + n_gat
    n_shift = 3 * dr + n_lora
    assert seq % CHUNK == 0 and dr % (2 * HEAD_DIM) == 0

    tn_a = _tile(dr, 512, LANES)
    lp = _round_up(n_lora, tn_a)
    assert (3 * dr) % lp == 0

    x2d = x.reshape(m, d)
    for layer in range(depth):
        wi = w_in[layer]
        w_a = jnp.concatenate(
            [wi[:, :3 * dr], jnp.pad(wi[:, 3 * dr:n_shift], ((0, 0), (0, lp - n_lora)))], axis=1).astype(BF16)
        mu_a = jnp.concatenate([mu_shift[layer][:3 * dr], jnp.pad(mu_shift[layer][3 * dr:], (0, lp - n_lora))])
        w_c = wi[:, n_shift:n_shift + 3 * dc].astype(BF16)
        w_g = wi[:, n_shift + 3 * dc:].astype(BF16)
        w_l = jnp.zeros((lp, 3 * dr), F32)
        w_l = w_l.at[:n_dec, :dr].set(w_lora_decay[layer])
        w_l = w_l.at[n_dec:n_dec + n_icl, dr:2 * dr].set(w_lora_iclr[layer])
        w_l = w_l.at[n_dec + n_icl:n_lora, 2 * dr:].set(w_lora_gate[layer]).astype(BF16)
        pv_l = jnp.concatenate([w0[layer], a0[layer], jnp.zeros((dr,), F32)]).reshape(1, 3 * dr)
        row = lambda p: p.reshape(1, dr)

        mod3 = _ada(c, w_ada[layer], b_ada[layer]).reshape(nb * N_ADA, 1, d)
        h = _norm_mod(x2d, norm1_gain[layer], mod3, 0, 1, seq)
        ps = _proj_shift(h, w_a, mu_a.reshape(1, -1), seq, 3 * dr // tn_a, tn_a, n_dec, n_icl)
        ycb = _proj_conv(h, w_c, conv_w_mix[layer], seq)
        sg = _proj_gate(h, w_g, seq)
        lig = _lora(ps, w_l, pv_l, seq, lp, 3 * dr // lp, dr)
        o = _scan(ps, lig, row(k_k[layer]), row(k_a[layer]), row(r_k[layer]), row(lnx_w[layer]),
                  row(lnx_b[layer]), nb, seq, dr)
        merged = _merge(o, ycb, w_o_rwkv[layer].astype(BF16), w_o_conv[layer].astype(BF16), sg, seq)
        x1 = _resid(merged, w_out[layer].astype(BF16), x2d, mod3, 2, seq, 1024, 1024, "attn_out", False)
        h2 = _norm_mod(x1, norm2_gain[layer], mod3, 3, 4, seq)
        act = _ffn_up(h2, w_ffn_up[layer].astype(BF16), conv_w_ffn[layer], seq)
        x2d = _resid(act, w_ffn_down[layer].astype(BF16), x1, mod3, 5, seq, 512, 512, "ffn_down", True)
    return _final_norm(x2d, final_gain).reshape(nb, seq, d)
```

```python
import functools

import jax
import jax.numpy as jnp
from jax import lax
from jax.experimental import pallas as pl
from jax.experimental.pallas import tpu as pltpu

F32 = jnp.float32
BF16 = jnp.bfloat16

NORM_EPS = 1e-6
LNX_EPS = 64e-5
HEAD_DIM = 64
N_ADA = 6
CONV_WIDTH = 3
LANES = 128
SUBLANES = 8
CHUNK = 64
VMEM_LIMIT = 56 * 1024 * 1024
LHS_PAD = LANES

_NN = (((1,), (0,)), ((), ()))
_NT = (((1,), (1,)), ((), ()))
_TN = (((0,), (0,)), ((), ()))


def _round_up(n, m):
    return (n + m - 1) // m * m


def _tile(n, pref, quantum):
    t = min(pref, n) // quantum * quantum
    while t > quantum and n % t:
        t -= quantum
    assert t > 0 and n % t == 0, (n, pref, quantum)
    return t


def _params(*sem):
    return pltpu.CompilerParams(dimension_semantics=sem, vmem_limit_bytes=VMEM_LIMIT)


def _ada_kernel(cb_ref, w_ref, b_ref, o_ref, *, nb, kd, tn):
    rep = tn // LANES

    def body(kc, accs):
        k0 = pl.multiple_of(kc * SUBLANES, SUBLANES)
        w = w_ref[pl.ds(k0, SUBLANES), :]
        new = []
        for b in range(nb):
            cv = cb_ref[b, pl.ds(k0, SUBLANES), :]
            cv = cv * jax.nn.sigmoid(cv)
            new.append(accs[b] + w * jnp.concatenate([cv] * rep, axis=1))
        return tuple(new)

    init = tuple(jnp.zeros((SUBLANES, tn), F32) for _ in range(nb))
    accs = lax.fori_loop(0, kd // SUBLANES, body, init, unroll=8)
    rows = [jnp.sum(a, axis=0, keepdims=True) for a in accs]
    o_ref[...] = jnp.concatenate(rows, axis=0) + b_ref[...]


def _ada(c, w, b):
    nb, kd = c.shape
    n = w.shape[1]
    tn = _tile(n, 1024, LANES)
    cb = jnp.broadcast_to(c[:, :, None], (nb, kd, LANES))
    return pl.pallas_call(
        functools.partial(_ada_kernel, nb=nb, kd=kd, tn=tn),
        out_shape=jax.ShapeDtypeStruct((nb, n), F32),
        grid=(n // tn,),
        in_specs=[pl.BlockSpec((nb, kd, LANES), lambda j: (0, 0, 0)),
                  pl.BlockSpec((kd, tn), lambda j: (0, j)),
                  pl.BlockSpec((1, tn), lambda j: (0, j))],
        out_specs=pl.BlockSpec((nb, tn), lambda j: (0, j)),
        compiler_params=_params("arbitrary"),
        name="ada",
    )(cb, w, b.reshape(1, n))


def _rms(x):
    return x * lax.rsqrt(jnp.mean(x * x, axis=-1, keepdims=True) + NORM_EPS)


def _norm_mod_kernel(x_ref, g_ref, sh_ref, sc_ref, o_ref):
    y = _rms(x_ref[...]) * g_ref[...]
    d = x_ref.shape[1]
    o_ref[:, :d] = (y * (1.0 + sc_ref[...]) + sh_ref[...]).astype(o_ref.dtype)
    o_ref[:, d:] = jnp.zeros((o_ref.shape[0], o_ref.shape[1] - d), o_ref.dtype)


def _norm_mod(x2d, gain, mod3, shift_idx, scale_idx, rows_per_batch):
    m, d = x2d.shape
    tm = _tile(rows_per_batch, 256, SUBLANES)
    tpb = rows_per_batch // tm
    return pl.pallas_call(
        _norm_mod_kernel,
        out_shape=jax.ShapeDtypeStruct((m, d + LHS_PAD), BF16),
        grid=(m // tm,),
        in_specs=[pl.BlockSpec((tm, d), lambda i: (i, 0)),
                  pl.BlockSpec((1, d), lambda i: (0, 0)),
                  pl.BlockSpec((None, 1, d), lambda i: ((i // tpb) * N_ADA + shift_idx, 0, 0)),
                  pl.BlockSpec((None, 1, d), lambda i: ((i // tpb) * N_ADA + scale_idx, 0, 0))],
        out_specs=pl.BlockSpec((tm, d + LHS_PAD), lambda i: (i, 0)),
        compiler_params=_params("arbitrary"),
        name="norm_mod",
    )(x2d, gain.reshape(1, d), mod3, mod3)


def _final_norm_kernel(x_ref, g_ref, o_ref):
    o_ref[...] = _rms(x_ref[...]) * g_ref[...]


def _final_norm(x2d, gain):
    m, d = x2d.shape
    tm = _tile(m, 256, SUBLANES)
    return pl.pallas_call(
        _final_norm_kernel,
        out_shape=jax.ShapeDtypeStruct((m, d), F32),
        grid=(m // tm,),
        in_specs=[pl.BlockSpec((tm, d), lambda i: (i, 0)),
                  pl.BlockSpec((1, d), lambda i: (0, 0))],
        out_specs=pl.BlockSpec((tm, d), lambda i: (i, 0)),
        compiler_params=_params("arbitrary"),
        name="final_norm",
    )(x2d, gain.reshape(1, d))


def _shift_rows(p, prev8, s):
    rolled = pltpu.roll(p, s, 0)
    row = lax.broadcasted_iota(jnp.int32, prev8.shape, 0)
    head = jnp.where(row < s, pltpu.roll(prev8, s, 0), rolled[:SUBLANES])
    return jnp.concatenate([head, rolled[SUBLANES:]], axis=0)


def _carry_swap(carry_ref, j, tail, first):
    prev8 = jnp.where(first, 0.0, carry_ref[j])
    carry_ref[j] = tail
    return prev8


def _causal_conv3(z, prev8, cw):
    return cw[0:1] * _shift_rows(z, prev8, 2) + cw[1:2] * _shift_rows(z, prev8, 1) + cw[2:3] * z


def _softplus(x):
    return jnp.maximum(x, 0.0) + jnp.log(1.0 + jnp.exp(-jnp.abs(x)))


def _proj_shift_kernel(h_ref, w_ref, mu_ref, o_ref, carry_ref, *, tpb, n_sub, lora_split):
    i, j = pl.program_id(0), pl.program_id(1)
    tn = o_ref.shape[1]
    ts = o_ref.shape[0] // n_sub
    mu = mu_ref[...]
    col = lax.broadcasted_iota(jnp.int32, (1, tn), 1)

    def finish(s, p, prev8):
        out = p + (_shift_rows(p, prev8, 1) - p) * mu
        if lora_split is not None:
            n_tanh, n_lin = lora_split
            out = jnp.where(col < n_tanh, jnp.tanh(out),
                            jnp.where(col < n_tanh + n_lin, out, jax.nn.sigmoid(out)))
        o_ref[s * ts:(s + 1) * ts, :] = out

    prev8 = jnp.where((i % tpb) == 0, 0.0, carry_ref[j])
    pending = None
    for s in range(n_sub):
        p = jnp.dot(h_ref[s * ts:(s + 1) * ts, :w_ref.shape[0]], w_ref[...], preferred_element_type=F32)
        if pending is not None:
            finish(*pending)
        pending = (s, p, prev8)
        prev8 = p[ts - SUBLANES:]
    finish(*pending)
    carry_ref[j] = prev8


def _proj_shift(h, w, mu, rows_per_batch, col0, n, tm_pref, tn, lora_split, name):
    m, dp = h.shape
    d = w.shape[0]
    tm = _tile(rows_per_batch, tm_pref, SUBLANES)
    j0 = col0 // tn
    return pl.pallas_call(
        functools.partial(_proj_shift_kernel, tpb=rows_per_batch // tm, n_sub=tm // _tile(tm, 256, SUBLANES),
                          lora_split=lora_split),
        out_shape=jax.ShapeDtypeStruct((m, n), F32),
        grid=(m // tm, n // tn),
        in_specs=[pl.BlockSpec((tm, dp), lambda i, j: (i, 0)),
                  pl.BlockSpec((d, tn), lambda i, j: (0, j0 + j)),
                  pl.BlockSpec((1, tn), lambda i, j: (0, j0 + j))],
        out_specs=pl.BlockSpec((tm, tn), lambda i, j: (i, j)),
        scratch_shapes=[pltpu.VMEM((n // tn, SUBLANES, tn), F32)],
        compiler_params=_params("arbitrary", "arbitrary"),
        name=name,
    )(h, w, mu)


def _proj_conv_kernel(h_ref, wb_ref, wc_ref, wx_ref, cw_ref, o_ref, carry_ref, *, tpb):
    i, j = pl.program_id(0), pl.program_id(1)
    tm = o_ref.shape[0]
    h = h_ref[:, :wb_ref.shape[0]]
    z = (jnp.dot(h, wc_ref[...], preferred_element_type=F32)
         * jnp.dot(h, wx_ref[...], preferred_element_type=F32))
    prev8 = _carry_swap(carry_ref, j, z[tm - SUBLANES:], (i % tpb) == 0)
    conv = _causal_conv3(z, prev8, cw_ref[...])
    o_ref[...] = (jnp.dot(h, wb_ref[...], preferred_element_type=F32) * conv).astype(o_ref.dtype)


def _proj_conv(h, w, cw, rows_per_batch):
    m, dp = h.shape
    d = w.shape[0]
    dc = cw.shape[1]
    tm = _tile(rows_per_batch, 1024, SUBLANES)
    tc = _tile(dc, 512, LANES)
    ncb = dc // tc
    wspec = lambda grp: pl.BlockSpec((d, tc), lambda i, j: (0, grp * ncb + j))
    return pl.pallas_call(
        functools.partial(_proj_conv_kernel, tpb=rows_per_batch // tm),
        out_shape=jax.ShapeDtypeStruct((m, dc), BF16),
        grid=(m // tm, ncb),
        in_specs=[pl.BlockSpec((tm, dp), lambda i, j: (i, 0)), wspec(0), wspec(1), wspec(2),
                  pl.BlockSpec((CONV_WIDTH, tc), lambda i, j: (0, j))],
        out_specs=pl.BlockSpec((tm, tc), lambda i, j: (i, j)),
        scratch_shapes=[pltpu.VMEM((ncb, SUBLANES, tc), F32)],
        compiler_params=_params("arbitrary", "arbitrary"),
        name="proj_conv",
    )(h, w, w, w, cw)


def _proj_gate_kernel(h_ref, w_ref, o_ref, *, n_sub):
    ts = o_ref.shape[0] // n_sub
    for s in range(n_sub):
        p = jnp.dot(h_ref[s * ts:(s + 1) * ts, :w_ref.shape[0]], w_ref[...], preferred_element_type=F32)
        o_ref[s * ts:(s + 1) * ts, :] = jax.nn.sigmoid(p).astype(o_ref.dtype)


def _proj_gate(h, w, rows_per_batch):
    m, dp = h.shape
    d = w.shape[0]
    n = w.shape[1]
    tm = _tile(rows_per_batch, 2048, SUBLANES)
    tn = _tile(n, 512, LANES)
    return pl.pallas_call(
        functools.partial(_proj_gate_kernel, n_sub=tm // _tile(tm, 256, SUBLANES)),
        out_shape=jax.ShapeDtypeStruct((m, n), BF16),
        grid=(m // tm, n // tn),
        in_specs=[pl.BlockSpec((tm, dp), lambda i, j: (i, 0)),
                  pl.BlockSpec((d, tn), lambda i, j: (0, j))],
        out_specs=pl.BlockSpec((tm, tn), lambda i, j: (i, j)),
        compiler_params=_params("arbitrary", "arbitrary"),
        name="proj_gate",
    )(h, w)


def _lora_kernel(a_ref, w_ref, pv_ref, o_ref, *, tiles_per_group):
    grp = pl.program_id(1) // tiles_per_group
    pre = jnp.dot(a_ref[...].astype(BF16), w_ref[...], preferred_element_type=F32) + pv_ref[...]

    @pl.when(grp == 0)
    def _():
        o_ref[...] = -jnp.exp(-_softplus(-pre) - 0.5)

    @pl.when(grp == 1)
    def _():
        o_ref[...] = jax.nn.sigmoid(pre)

    @pl.when(grp == 2)
    def _():
        o_ref[...] = pre


def _lora(ps, w, pv, rows_per_batch, lp, col_block, dr):
    m = ps.shape[0]
    n = w.shape[1]
    tm = _tile(rows_per_batch, 1024, SUBLANES)
    tn = _tile(dr, 2048, LANES)
    return pl.pallas_call(
        functools.partial(_lora_kernel, tiles_per_group=dr // tn),
        out_shape=jax.ShapeDtypeStruct((m, n), F32),
        grid=(m // tm, n // tn),
        in_specs=[pl.BlockSpec((tm, lp), lambda i, j: (i, col_block)),
                  pl.BlockSpec((lp, tn), lambda i, j: (0, j)),
                  pl.BlockSpec((1, tn), lambda i, j: (0, j))],
        out_specs=pl.BlockSpec((tm, tn), lambda i, j: (i, j)),
        compiler_params=_params("arbitrary", "arbitrary"),
        name="lora",
    )(ps, w, pv)


def _split(x, n):
    pieces = []
    for _ in range(n):
        p = x.astype(BF16)
        pieces.append(p)
        x = x - p.astype(F32)
    return pieces


def _dot(a, b, dims, na=1, nb=1):
    pa = [a] if a.dtype == BF16 else _split(a, na)
    pb = [b] if b.dtype == BF16 else _split(b, nb)
    depth = max(len(pa), len(pb))
    acc = None
    for ia, xa in enumerate(pa):
        for ib, xb in enumerate(pb):
            if ia + ib >= depth:
                continue
            d = lax.dot_general(xa, xb, dims, preferred_element_type=F32)
            acc = d if acc is None else acc + d
    return acc


def _scan_kernel(r_ref, k_ref, v_ref, lw_ref, ic_ref, g_ref, kk_ref, ka_ref, rk_ref, lnw_ref, lnb_ref,
                 o_ref, st_ref, *, nb, pairs):
    cl = CHUNK
    pw = 2 * HEAD_DIM
    chains = [(b, pp) for b in range(nb) for pp in range(pairs)]
    nch = len(chains)

    @pl.when(pl.program_id(1) == 0)
    def _():
        st_ref[...] = jnp.zeros_like(st_ref)

    ri = lax.broadcasted_iota(jnp.int32, (2 * cl, 2 * cl), 0)
    ci = lax.broadcasted_iota(jnp.int32, (2 * cl, 2 * cl), 1)
    strict = ri > ci
    incl = ri >= ci
    eye = jnp.where(ri == ci, 1.0, 0.0).astype(F32)
    hi = lax.broadcasted_iota(jnp.int32, (pw, pw), 0) // HEAD_DIM
    hj = lax.broadcasted_iota(jnp.int32, (pw, pw), 1) // HEAD_DIM
    head_ones = jnp.where(hi == hj, 1.0, 0.0).astype(BF16)
    ti = lax.broadcasted_iota(jnp.int32, (cl, cl), 0)
    tj = lax.broadcasted_iota(jnp.int32, (cl, cl), 1)
    cum_ones = jnp.where(ti >= tj, 1.0, 0.0).astype(BF16)
    first_head = lax.broadcasted_iota(jnp.int32, (1, pw), 1) < HEAD_DIM
    inv_n = 1.0 / HEAD_DIM

    def block_diag(x):
        return jnp.concatenate([jnp.where(first_head, x, 0.0), jnp.where(first_head, 0.0, x)], axis=0)

    def head_sums(xs):
        s = _dot(jnp.concatenate(xs, axis=0), head_ones, _NN)
        return [s[i * cl:(i + 1) * cl] for i in range(nch)]

    def tok(ref):
        return [ref[b, :, pp * pw:(pp + 1) * pw] for b, pp in chains]

    def par(ref):
        return [ref[:, pp * pw:(pp + 1) * pw] for _, pp in chains]

    def each(f, *ls):
        return [f(*xs) for xs in zip(*ls)]

    def rows(*xs):
        return jnp.concatenate(xs, axis=0)

    def cols(*xs):
        return jnp.concatenate(xs, axis=1)

    r, kraw, v, lw, ic, g = tok(r_ref), tok(k_ref), tok(v_ref), tok(lw_ref), tok(ic_ref), tok(g_ref)

    kk = each(lambda k, p: k * p, kraw, par(kk_ref))
    ss = head_sums(each(lambda x: x * x, kk))
    kk = each(lambda x, s: x / jnp.maximum(jnp.sqrt(s), 1e-12), kk, ss)
    kmod = each(lambda k, i, p: k * (1.0 + (i - 1.0) * p), kraw, ic, par(ka_ref))
    bvec = each(lambda x, i: x * i, kk, ic)

    lg_b = [_dot(cum_ones, lw_ref[b], _NN, nb=2) for b in range(nb)]
    lg = [lg_b[b][:, pp * pw:(pp + 1) * pw] for b, pp in chains]
    gcol = each(lambda x: jnp.transpose(jnp.broadcast_to(jnp.exp(x[cl - 1:cl, :]), (pw, pw))), lg)
    g_inv = each(lambda x: jnp.exp(-x), lg)
    g_end = each(lambda x: jnp.exp(x[cl - 1:cl, :] - x), lg)

    a_bd = each(lambda x, l, w: block_diag(-x * jnp.exp(l - w)), kk, lg, lw)
    r_bd = each(lambda x, l: block_diag(x * jnp.exp(l)), r, lg)
    b_bd = each(lambda x, gi: block_diag(x * gi), bvec, g_inv)
    k_bd = each(lambda x, gi: block_diag(x * gi), kmod, g_inv)
    v_bd = each(block_diag, v)
    bend_bd = each(lambda x, ge: block_diag(x * ge), bvec, g_end)
    kend_bd = each(lambda x, ge: block_diag(x * ge), kmod, g_end)

    sc = each(lambda a, rr, b, k: _dot(jnp.concatenate([a, rr], axis=0), jnp.concatenate([b, k], axis=0), _NT),
              a_bd, r_bd, b_bd, k_bd)
    l_ab = each(lambda s: jnp.where(strict, s[:2 * cl, :2 * cl], 0.0), sc)
    m_ak = each(lambda s: jnp.where(strict, s[:2 * cl, 2 * cl:], 0.0), sc)
    m_rb = each(lambda s: jnp.where(incl, s[2 * cl:, :2 * cl], 0.0), sc)
    m_rk = each(lambda s: jnp.where(incl, s[2 * cl:, 2 * cl:], 0.0), sc)

    n_sq = cl.bit_length() - 2
    tinv = each(lambda l: eye + l, l_ab)
    lpow = each(lambda p: _dot(p, p, _NN), l_ab)
    for _ in range(n_sq - 1):
        z = each(lambda p, t: _dot(rows(p, t), p, _NN), lpow, tinv)
        lpow = each(lambda zz: zz[:2 * cl], z)
        tinv = each(lambda t, zz: t + zz[2 * cl:], tinv, z)
    tinv = each(lambda t, p: t + _dot(t, p, _NN), tinv, lpow)

    st = [st_ref[i] for i in range(nch)]
    rhs = each(lambda a, mm, s, vv: _dot(cols(a, mm), rows(s, vv), _NN), a_bd, m_ak, st, v_bd)
    u = each(lambda t, x: _dot(t, x, _NN), tinv, rhs)
    y_bd = each(lambda rr, mb, mk, s, uu, vv: _dot(cols(rr, mb, mk), rows(s, uu, vv), _NN),
                r_bd, m_rb, m_rk, st, u, v_bd)
    for i in range(nch):
        st_ref[i] = gcol[i] * st[i] + _dot(rows(bend_bd[i], kend_bd[i]), rows(u[i], v_bd[i]), _TN)
    y = each(lambda x: x[:cl] + x[cl:], y_bd)

    mean = each(lambda s: s * inv_n, head_sums(y))
    dy = each(lambda a, b: a - b, y, mean)
    var = each(lambda s: s * inv_n, head_sums(each(lambda x: x * x, dy)))
    bon = head_sums(each(lambda a, b, p: a * b * p, r, kmod, par(rk_ref)))
    for i, (b, pp) in enumerate(chains):
        sl = slice(pp * pw, (pp + 1) * pw)
        yn = dy[i] * lax.rsqrt(var[i] + LNX_EPS) * lnw_ref[:, sl] + lnb_ref[:, sl]
        o_ref[b, :, sl] = ((yn + bon[i] * v[i]) * g[i]).astype(o_ref.dtype)


def _scan(ps, lig, k_k, k_a, r_k, lnx_w, lnx_b, nb, seq, dr):
    pw = 2 * HEAD_DIM
    pairs = 8 if dr % (8 * pw) == 0 else 1
    bw = pairs * pw
    npb = dr // bw
    nc = seq // CHUNK
    ps3 = ps.reshape(nb, seq, ps.shape[1])
    lig3 = lig.reshape(nb, seq, lig.shape[1])
    tok = lambda off: pl.BlockSpec((nb, CHUNK, bw), lambda p, c: (0, c, off * npb + p))
    par = pl.BlockSpec((1, bw), lambda p, c: (0, p))
    o = pl.pallas_call(
        functools.partial(_scan_kernel, nb=nb, pairs=pairs),
        out_shape=jax.ShapeDtypeStruct((nb, seq, dr), BF16),
        grid=(npb, nc),
        in_specs=[tok(0), tok(1), tok(2), tok(0), tok(1), tok(2), par, par, par, par, par],
        out_specs=pl.BlockSpec((nb, CHUNK, bw), lambda p, c: (0, c, p)),
        scratch_shapes=[pltpu.VMEM((nb * pairs, pw, pw), F32)],
        compiler_params=_params("arbitrary", "arbitrary"),
        name="rwkv_scan",
    )(ps3, ps3, ps3, lig3, lig3, lig3, k_k, k_a, r_k, lnx_w, lnx_b)
    return o.reshape(nb * seq, dr)


def _merge_kernel(o_ref, y_ref, wa_ref, wb_ref, sa_ref, sb_ref, out_ref):
    ya = jnp.dot(o_ref[...], wa_ref[...], preferred_element_type=F32)
    yb = jnp.dot(y_ref[...], wb_ref[...], preferred_element_type=F32)
    out_ref[...] = (sa_ref[...].astype(F32) * ya + sb_ref[...].astype(F32) * yb).astype(out_ref.dtype)


def _merge(o, ycb, wa, wb, sg, rows_per_batch):
    m, dr = o.shape
    dc = ycb.shape[1]
    d = wa.shape[1]
    tm = _tile(rows_per_batch, 1024, SUBLANES)
    tn = _tile(d, 1024, LANES)
    nj = d // tn
    return pl.pallas_call(
        _merge_kernel,
        out_shape=jax.ShapeDtypeStruct((m, d), BF16),
        grid=(m // tm, nj),
        in_specs=[pl.BlockSpec((tm, dr), lambda i, j: (i, 0)),
                  pl.BlockSpec((tm, dc), lambda i, j: (i, 0)),
                  pl.BlockSpec((dr, tn), lambda i, j: (0, j)),
                  pl.BlockSpec((dc, tn), lambda i, j: (0, j)),
                  pl.BlockSpec((tm, tn), lambda i, j: (i, j)),
                  pl.BlockSpec((tm, tn), lambda i, j: (i, nj + j))],
        out_specs=pl.BlockSpec((tm, tn), lambda i, j: (i, j)),
        compiler_params=_params("arbitrary", "arbitrary"),
        name="merge",
    )(o, ycb, wa, wb, sg, sg)


def _resid_kernel(a_ref, w_ref, x_ref, gt_ref, o_ref):
    y = jnp.dot(a_ref[...], w_ref[...], preferred_element_type=F32)
    o_ref[...] = x_ref[...] + gt_ref[...] * y


def _resid(a, w, x2d, mod3, gate_idx, rows_per_batch, tm_pref, tn_pref, name, cols_outer):
    m, kd = a.shape
    d = w.shape[1]
    tm = _tile(rows_per_batch, tm_pref, SUBLANES)
    tn = _tile(d, tn_pref, LANES)
    tpb = rows_per_batch // tm
    if cols_outer:
        grid = (d // tn, m // tm)
        ij = lambda f: (lambda j, i: f(i, j))
    else:
        grid = (m // tm, d // tn)
        ij = lambda f: f
    return pl.pallas_call(
        _resid_kernel,
        out_shape=jax.ShapeDtypeStruct((m, d), F32),
        grid=grid,
        in_specs=[pl.BlockSpec((tm, kd), ij(lambda i, j: (i, 0))),
                  pl.BlockSpec((kd, tn), ij(lambda i, j: (0, j))),
                  pl.BlockSpec((tm, tn), ij(lambda i, j: (i, j))),
                  pl.BlockSpec((None, 1, tn), ij(lambda i, j: ((i // tpb) * N_ADA + gate_idx, 0, j)))],
        out_specs=pl.BlockSpec((tm, tn), ij(lambda i, j: (i, j))),
        compiler_params=_params("arbitrary", "arbitrary"),
        name=name,
    )(a, w, x2d, mod3)


def _ffn_up_kernel(h_ref, wg_ref, wv_ref, cw_ref, o_ref, carry_ref, *, tpb, n_sub):
    i, j = pl.program_id(0), pl.program_id(1)
    ts = o_ref.shape[0] // n_sub
    cw = cw_ref[...]

    def finish(s, gate, val, prev8):
        gc = _causal_conv3(gate, prev8, cw)
        o_ref[s * ts:(s + 1) * ts, :] = (gc * jax.nn.sigmoid(gc) * val).astype(o_ref.dtype)

    prev8 = jnp.where((i % tpb) == 0, 0.0, carry_ref[j])
    pending = None
    for s in range(n_sub):
        h = h_ref[s * ts:(s + 1) * ts, :wg_ref.shape[0]]
        gate = jnp.dot(h, wg_ref[...], preferred_element_type=F32)
        val = jnp.dot(h, wv_ref[...], preferred_element_type=F32)
        if pending is not None:
            finish(*pending)
        pending = (s, gate, val, prev8)
        prev8 = gate[ts - SUBLANES:]
    finish(*pending)
    carry_ref[j] = prev8


def _ffn_up(h, w, cw, rows_per_batch):
    m, dp = h.shape
    d = w.shape[0]
    dff = cw.shape[1]
    tm = _tile(rows_per_batch, 2048, SUBLANES)
    tc = _tile(dff, 256, LANES)
    ncb = dff // tc
    return pl.pallas_call(
        functools.partial(_ffn_up_kernel, tpb=rows_per_batch // tm, n_sub=tm // _tile(tm, 128, SUBLANES)),
        out_shape=jax.ShapeDtypeStruct((m, dff), BF16),
        grid=(m // tm, ncb),
        in_specs=[pl.BlockSpec((tm, dp), lambda i, j: (i, 0)),
                  pl.BlockSpec((d, tc), lambda i, j: (0, j)),
                  pl.BlockSpec((d, tc), lambda i, j: (0, ncb + j)),
                  pl.BlockSpec((CONV_WIDTH, tc), lambda i, j: (0, j))],
        out_specs=pl.BlockSpec((tm, tc), lambda i, j: (i, j)),
        scratch_shapes=[pltpu.VMEM((ncb, SUBLANES, tc), F32)],
        compiler_params=_params("arbitrary", "arbitrary"),
        name="ffn_up",
    )(h, w, w, cw)


def kernel(x, c, w_ada, b_ada, norm1_gain, w_in, mu_shift, w0, a0, k_k, k_a, r_k, w_lora_decay,
           w_lora_iclr, w_lora_gate, lnx_w, lnx_b, conv_w_mix, w_o_rwkv, w_o_conv, w_out, norm2_gain,
           w_ffn_up, conv_w_ffn, w_ffn_down, final_gain):
    nb, seq, d = x.shape
    m = nb * seq
    depth = w_ada.shape[0]
    dr = w_o_rwkv.shape[1]
    dc = w_o_conv.shape[1]
    n_dec, n_icl, n_gat = w_lora_decay.shape[1], w_lora_iclr.shape[1], w_lora_gate.shape[1]
    n_lora = n_dec + n_icl + n_gat
    n_shift = 3 * dr + n_lora
    assert seq % CHUNK == 0 and dr % (2 * HEAD_DIM) == 0

    lp = _round_up(n_lora, _tile(dr, 512, LANES))
    assert (3 * dr) % lp == 0 and w_in.shape[2] >= 3 * dr + lp

    x2d = x.reshape(m, d)
    for layer in range(depth):
        wi = w_in[layer].astype(BF16)
        mu_a = jnp.pad(mu_shift[layer], (0, 3 * dr + lp - n_shift)).reshape(1, -1)
        w_c = wi[:, n_shift:n_shift + 3 * dc]
        w_g = wi[:, n_shift + 3 * dc:]
        w_l = jnp.zeros((lp, 3 * dr), F32)
        w_l = w_l.at[:n_dec, :dr].set(w_lora_decay[layer])
        w_l = w_l.at[n_dec:n_dec + n_icl, dr:2 * dr].set(w_lora_iclr[layer])
        w_l = w_l.at[n_dec + n_icl:n_lora, 2 * dr:].set(w_lora_gate[layer]).astype(BF16)
        pv_l = jnp.concatenate([w0[layer], a0[layer], jnp.zeros((dr,), F32)]).reshape(1, 3 * dr)
        row = lambda p: p.reshape(1, dr)

        mod3 = _ada(c, w_ada[layer], b_ada[layer]).reshape(nb * N_ADA, 1, d)
        h = _norm_mod(x2d, norm1_gain[layer], mod3, 0, 1, seq)
        ps = _proj_shift(h, wi, mu_a, seq, 0, 3 * dr, 2048, _tile(dr, 512, LANES), None, "proj_rkv")
        la = _proj_shift(h, wi, mu_a, seq, 3 * dr, lp, 1024, lp, (n_dec, n_icl), "proj_lora")
        ycb = _proj_conv(h, w_c, conv_w_mix[layer], seq)
        sg = _proj_gate(h, w_g, seq)
        lig = _lora(la, w_l, pv_l, seq, lp, 0, dr)
        o = _scan(ps, lig, row(k_k[layer]), row(k_a[layer]), row(r_k[layer]), row(lnx_w[layer]),
                  row(lnx_b[layer]), nb, seq, dr)
        merged = _merge(o, ycb, w_o_rwkv[layer].astype(BF16), w_o_conv[layer].astype(BF16), sg, seq)
        x1 = _resid(merged, w_out[layer].astype(BF16), x2d, mod3, 2, seq, 1024, 1024, "attn_out", False)
        h2 = _norm_mod(x1, norm2_gain[layer], mod3, 3, 4, seq)
        act = _ffn_up(h2, w_ffn_up[layer].astype(BF16), conv_w_ffn[layer], seq)
        x2d = _resid(act, w_ffn_down[layer].astype(BF16), x1, mod3, 5, seq, 512, 512, "ffn_down", True)
    return _final_norm(x2d, final_gain).reshape(nb, seq, d)
```

```python
import functools

import jax
import jax.numpy as jnp
from jax import lax
from jax.experimental import pallas as pl
from jax.experimental.pallas import tpu as pltpu

F32 = jnp.float32
BF16 = jnp.bfloat16

NORM_EPS = 1e-6
LNX_EPS = 64e-5
DECAY_SCALE = 0.6065306597126334
HEAD_DIM = 64
N_ADA = 6
CONV_WIDTH = 3
LANES = 128
SUBLANES = 8
CHUNK = 64
VMEM_LIMIT = 56 * 1024 * 1024
LHS_PAD = LANES

_NN = (((1,), (0,)), ((), ()))
_NT = (((1,), (1,)), ((), ()))
_TN = (((0,), (0,)), ((), ()))


def _round_up(n, m):
    return (n + m - 1) // m * m


def _tile(n, pref, quantum):
    t = min(pref, n) // quantum * quantum
    while t > quantum and n % t:
        t -= quantum
    assert t > 0 and n % t == 0, (n, pref, quantum)
    return t


def _params(*sem):
    return pltpu.CompilerParams(dimension_semantics=sem, vmem_limit_bytes=VMEM_LIMIT)


def _ada_kernel(cb_ref, w_ref, b_ref, o_ref, *, nb, kd, tn):
    rep = tn // LANES

    def body(kc, accs):
        k0 = pl.multiple_of(kc * SUBLANES, SUBLANES)
        w = w_ref[pl.ds(k0, SUBLANES), :]
        new = []
        for b in range(nb):
            cv = cb_ref[b, pl.ds(k0, SUBLANES), :]
            cv = cv * jax.nn.sigmoid(cv)
            new.append(accs[b] + w * jnp.concatenate([cv] * rep, axis=1))
        return tuple(new)

    init = tuple(jnp.zeros((SUBLANES, tn), F32) for _ in range(nb))
    accs = lax.fori_loop(0, kd // SUBLANES, body, init, unroll=8)
    rows = [jnp.sum(a, axis=0, keepdims=True) for a in accs]
    o_ref[...] = jnp.concatenate(rows, axis=0) + b_ref[...]


def _ada(c, w, b):
    nb, kd = c.shape
    n = w.shape[1]
    tn = _tile(n, 1024, LANES)
    cb = jnp.broadcast_to(c[:, :, None], (nb, kd, LANES))
    return pl.pallas_call(
        functools.partial(_ada_kernel, nb=nb, kd=kd, tn=tn),
        out_shape=jax.ShapeDtypeStruct((nb, n), F32),
        grid=(n // tn,),
        in_specs=[pl.BlockSpec((nb, kd, LANES), lambda j: (0, 0, 0)),
                  pl.BlockSpec((kd, tn), lambda j: (0, j)),
                  pl.BlockSpec((1, tn), lambda j: (0, j))],
        out_specs=pl.BlockSpec((nb, tn), lambda j: (0, j)),
        compiler_params=_params("arbitrary"),
        name="ada",
    )(cb, w, b.reshape(1, n))


def _rms(x):
    return x * lax.rsqrt(jnp.mean(x * x, axis=-1, keepdims=True) + NORM_EPS)


def _norm_mod_kernel(x_ref, g_ref, sh_ref, sc_ref, o_ref):
    y = _rms(x_ref[...]) * g_ref[...]
    d = x_ref.shape[1]
    o_ref[:, :d] = (y * (1.0 + sc_ref[...]) + sh_ref[...]).astype(o_ref.dtype)
    o_ref[:, d:] = jnp.zeros((o_ref.shape[0], o_ref.shape[1] - d), o_ref.dtype)


def _norm_mod(x2d, gain, mod3, shift_idx, scale_idx, rows_per_batch):
    m, d = x2d.shape
    tm = _tile(rows_per_batch, 512, SUBLANES)
    tpb = rows_per_batch // tm
    return pl.pallas_call(
        _norm_mod_kernel,
        out_shape=jax.ShapeDtypeStruct((m, d + LHS_PAD), BF16),
        grid=(m // tm,),
        in_specs=[pl.BlockSpec((tm, d), lambda i: (i, 0)),
                  pl.BlockSpec((1, d), lambda i: (0, 0)),
                  pl.BlockSpec((None, 1, d), lambda i: ((i // tpb) * N_ADA + shift_idx, 0, 0)),
                  pl.BlockSpec((None, 1, d), lambda i: ((i // tpb) * N_ADA + scale_idx, 0, 0))],
        out_specs=pl.BlockSpec((tm, d + LHS_PAD), lambda i: (i, 0)),
        compiler_params=_params("arbitrary"),
        name="norm_mod",
    )(x2d, gain.reshape(1, d), mod3, mod3)


def _final_norm_kernel(x_ref, g_ref, o_ref):
    o_ref[...] = _rms(x_ref[...]) * g_ref[...]


def _final_norm(x2d, gain):
    m, d = x2d.shape
    tm = _tile(m, 512, SUBLANES)
    return pl.pallas_call(
        _final_norm_kernel,
        out_shape=jax.ShapeDtypeStruct((m, d), F32),
        grid=(m // tm,),
        in_specs=[pl.BlockSpec((tm, d), lambda i: (i, 0)),
                  pl.BlockSpec((1, d), lambda i: (0, 0))],
        out_specs=pl.BlockSpec((tm, d), lambda i: (i, 0)),
        compiler_params=_params("arbitrary"),
        name="final_norm",
    )(x2d, gain.reshape(1, d))


def _shift_rows(p, prev8, s):
    rolled = pltpu.roll(p, s, 0)
    row = lax.broadcasted_iota(jnp.int32, prev8.shape, 0)
    head = jnp.where(row < s, pltpu.roll(prev8, s, 0), rolled[:SUBLANES])
    return jnp.concatenate([head, rolled[SUBLANES:]], axis=0)


def _carry_swap(carry_ref, j, tail, first):
    prev8 = jnp.where(first, 0.0, carry_ref[j])
    carry_ref[j] = tail
    return prev8


def _causal_conv3(z, prev8, cw):
    return cw[0:1] * _shift_rows(z, prev8, 2) + cw[1:2] * _shift_rows(z, prev8, 1) + cw[2:3] * z


def _proj_shift_kernel(h_ref, w_ref, mu_ref, o_ref, carry_ref, *, tpb, n_sub, lora_split):
    i, j = pl.program_id(0), pl.program_id(1)
    tn = o_ref.shape[1]
    ts = o_ref.shape[0] // n_sub
    mu = mu_ref[...]
    col = lax.broadcasted_iota(jnp.int32, (1, tn), 1)

    def finish(s, p, prev8):
        out = p + (_shift_rows(p, prev8, 1) - p) * mu
        if lora_split is not None:
            n_tanh, n_lin = lora_split
            out = jnp.where(col < n_tanh, jnp.tanh(out),
                            jnp.where(col < n_tanh + n_lin, out, jax.nn.sigmoid(out)))
        o_ref[s * ts:(s + 1) * ts, :] = out

    prev8 = jnp.where((i % tpb) == 0, 0.0, carry_ref[j])
    pending = None
    for s in range(n_sub):
        p = jnp.dot(h_ref[s * ts:(s + 1) * ts, :w_ref.shape[0]], w_ref[...], preferred_element_type=F32)
        if pending is not None:
            finish(*pending)
        pending = (s, p, prev8)
        prev8 = p[ts - SUBLANES:]
    finish(*pending)
    carry_ref[j] = prev8


def _proj_shift(h, w, mu, rows_per_batch, col0, n, tm_pref, tn, lora_split, name):
    m, dp = h.shape
    d = w.shape[0]
    tm = _tile(rows_per_batch, tm_pref, SUBLANES)
    j0 = col0 // tn
    return pl.pallas_call(
        functools.partial(_proj_shift_kernel, tpb=rows_per_batch // tm, n_sub=tm // _tile(tm, 256, SUBLANES),
                          lora_split=lora_split),
        out_shape=jax.ShapeDtypeStruct((m, n), F32),
        grid=(m // tm, n // tn),
        in_specs=[pl.BlockSpec((tm, dp), lambda i, j: (i, 0)),
                  pl.BlockSpec((d, tn), lambda i, j: (0, j0 + j)),
                  pl.BlockSpec((1, tn), lambda i, j: (0, j0 + j))],
        out_specs=pl.BlockSpec((tm, tn), lambda i, j: (i, j)),
        scratch_shapes=[pltpu.VMEM((n // tn, SUBLANES, tn), F32)],
        compiler_params=_params("arbitrary", "arbitrary"),
        name=name,
    )(h, w, mu)


def _proj_conv_kernel(h_ref, wb_ref, wc_ref, wx_ref, cw_ref, o_ref, carry_ref, *, tpb):
    i, j = pl.program_id(0), pl.program_id(1)
    tm = o_ref.shape[0]
    h = h_ref[:, :wb_ref.shape[0]]
    z = (jnp.dot(h, wc_ref[...], preferred_element_type=F32)
         * jnp.dot(h, wx_ref[...], preferred_element_type=F32))
    prev8 = _carry_swap(carry_ref, j, z[tm - SUBLANES:], (i % tpb) == 0)
    conv = _causal_conv3(z, prev8, cw_ref[...])
    o_ref[...] = (jnp.dot(h, wb_ref[...], preferred_element_type=F32) * conv).astype(o_ref.dtype)


def _proj_conv(h, w, cw, rows_per_batch):
    m, dp = h.shape
    d = w.shape[0]
    dc = cw.shape[1]
    tm = _tile(rows_per_batch, 1024, SUBLANES)
    tc = _tile(dc, 512, LANES)
    ncb = dc // tc
    wspec = lambda grp: pl.BlockSpec((d, tc), lambda i, j: (0, grp * ncb + j))
    return pl.pallas_call(
        functools.partial(_proj_conv_kernel, tpb=rows_per_batch // tm),
        out_shape=jax.ShapeDtypeStruct((m, dc), BF16),
        grid=(m // tm, ncb),
        in_specs=[pl.BlockSpec((tm, dp), lambda i, j: (i, 0)), wspec(0), wspec(1), wspec(2),
                  pl.BlockSpec((CONV_WIDTH, tc), lambda i, j: (0, j))],
        out_specs=pl.BlockSpec((tm, tc), lambda i, j: (i, j)),
        scratch_shapes=[pltpu.VMEM((ncb, SUBLANES, tc), F32)],
        compiler_params=_params("arbitrary", "arbitrary"),
        name="proj_conv",
    )(h, w, w, w, cw)


def _proj_gate_kernel(h_ref, w_ref, o_ref, *, n_sub):
    ts = o_ref.shape[0] // n_sub
    for s in range(n_sub):
        p = jnp.dot(h_ref[s * ts:(s + 1) * ts, :w_ref.shape[0]], w_ref[...], preferred_element_type=F32)
        o_ref[s * ts:(s + 1) * ts, :] = jax.nn.sigmoid(p).astype(o_ref.dtype)


def _proj_gate(h, w, rows_per_batch):
    m, dp = h.shape
    d = w.shape[0]
    n = w.shape[1]
    tm = _tile(rows_per_batch, 2048, SUBLANES)
    tn = _tile(n, 512, LANES)
    return pl.pallas_call(
        functools.partial(_proj_gate_kernel, n_sub=tm // _tile(tm, 256, SUBLANES)),
        out_shape=jax.ShapeDtypeStruct((m, n), BF16),
        grid=(m // tm, n // tn),
        in_specs=[pl.BlockSpec((tm, dp), lambda i, j: (i, 0)),
                  pl.BlockSpec((d, tn), lambda i, j: (0, j))],
        out_specs=pl.BlockSpec((tm, tn), lambda i, j: (i, j)),
        compiler_params=_params("arbitrary", "arbitrary"),
        name="proj_gate",
    )(h, w)


def _lora_kernel(a_ref, w_ref, pv_ref, o_ref, *, tiles_per_group):
    grp = pl.program_id(1) // tiles_per_group
    pre = jnp.dot(a_ref[...].astype(BF16), w_ref[...], preferred_element_type=F32) + pv_ref[...]

    @pl.when(grp == 0)
    def _():
        o_ref[...] = -DECAY_SCALE * jax.nn.sigmoid(pre)

    @pl.when(grp == 1)
    def _():
        o_ref[...] = jax.nn.sigmoid(pre)

    @pl.when(grp == 2)
    def _():
        o_ref[...] = pre


def _lora(ps, w, pv, rows_per_batch, lp, col_block, dr):
    m = ps.shape[0]
    n = w.shape[1]
    tm = _tile(rows_per_batch, 1024, SUBLANES)
    tn = _tile(dr, 2048, LANES)
    return pl.pallas_call(
        functools.partial(_lora_kernel, tiles_per_group=dr // tn),
        out_shape=jax.ShapeDtypeStruct((m, n), F32),
        grid=(m // tm, n // tn),
        in_specs=[pl.BlockSpec((tm, lp), lambda i, j: (i, col_block)),
                  pl.BlockSpec((lp, tn), lambda i, j: (0, j)),
                  pl.BlockSpec((1, tn), lambda i, j: (0, j))],
        out_specs=pl.BlockSpec((tm, tn), lambda i, j: (i, j)),
        compiler_params=_params("arbitrary", "arbitrary"),
        name="lora",
    )(ps, w, pv)


def _split(x, n):
    pieces = []
    for _ in range(n):
        p = x.astype(BF16)
        pieces.append(p)
        x = x - p.astype(F32)
    return pieces


def _dot(a, b, dims, na=1, nb=1):
    pa = [a] if a.dtype == BF16 else _split(a, na)
    pb = [b] if b.dtype == BF16 else _split(b, nb)
    depth = max(len(pa), len(pb))
    acc = None
    for ia, xa in enumerate(pa):
        for ib, xb in enumerate(pb):
            if ia + ib >= depth:
                continue
            d = lax.dot_general(xa, xb, dims, preferred_element_type=F32)
            acc = d if acc is None else acc + d
    return acc


def _scan_kernel(r_ref, k_ref, v_ref, lw_ref, ic_ref, g_ref, kk_ref, ka_ref, rk_ref, lnw_ref, lnb_ref,
                 o_ref, st_ref, *, nb, pairs):
    cl = CHUNK
    pw = 2 * HEAD_DIM

    @pl.when(pl.program_id(1) == 0)
    def _():
        st_ref[...] = jnp.zeros_like(st_ref)

    ri = lax.broadcasted_iota(jnp.int32, (2 * cl, 2 * cl), 0)
    ci = lax.broadcasted_iota(jnp.int32, (2 * cl, 2 * cl), 1)
    strict = ri > ci
    incl = ri >= ci
    eye = jnp.where(ri == ci, 1.0, 0.0).astype(F32)
    hi = lax.broadcasted_iota(jnp.int32, (pw, pw), 0) // HEAD_DIM
    hj = lax.broadcasted_iota(jnp.int32, (pw, pw), 1) // HEAD_DIM
    head_ones = jnp.where(hi == hj, 1.0, 0.0).astype(BF16)
    ti = lax.broadcasted_iota(jnp.int32, (cl, cl), 0)
    tj = lax.broadcasted_iota(jnp.int32, (cl, cl), 1)
    cum_ones = jnp.where(ti >= tj, 1.0, 0.0).astype(BF16)
    first_head = lax.broadcasted_iota(jnp.int32, (1, pw), 1) < HEAD_DIM
    inv_n = 1.0 / HEAD_DIM

    def block_diag(x):
        return jnp.concatenate([jnp.where(first_head, x, 0.0), jnp.where(first_head, 0.0, x)], axis=0)

    def head_sums(xs):
        s = _dot(jnp.concatenate(xs, axis=0), head_ones, _NN)
        return [s[i * cl:(i + 1) * cl] for i in range(len(xs))]

    def each(f, *ls):
        return [f(*xs) for xs in zip(*ls)]

    def rows(*xs):
        return jnp.concatenate(xs, axis=0)

    def cols(*xs):
        return jnp.concatenate(xs, axis=1)

    def prep(grp, q):
        tok = lambda ref: [ref[b, :, pp * pw:(pp + 1) * pw] for b, pp in grp]
        par = lambda ref: [ref[:, pp * pw:(pp + 1) * pw] for _, pp in grp]
        r, kraw, v, lw, ic = tok(r_ref), tok(k_ref), tok(v_ref), tok(lw_ref), tok(ic_ref)
        kk = each(lambda k, p: k * p, kraw, par(kk_ref))
        ss = head_sums(each(lambda x: x * x, kk))
        lg = each(lambda w: _dot(cum_ones, w, _NN, nb=2), lw)
        yield
        kk = each(lambda x, s: x / jnp.maximum(jnp.sqrt(s), 1e-12), kk, ss)
        kmod = each(lambda k, i, p: k * (1.0 + (i - 1.0) * p), kraw, ic, par(ka_ref))
        bvec = each(lambda x, i: x * i, kk, ic)
        yield
        g_inv = each(lambda x: jnp.exp(-x), lg)
        g_end = each(lambda x: jnp.exp(x[cl - 1:cl, :] - x), lg)
        q["gcol"] = each(lambda x: jnp.transpose(jnp.broadcast_to(jnp.exp(x[cl - 1:cl, :]), (pw, pw))), lg)
        yield
        q["a_bd"] = each(lambda x, l, w: block_diag(-x * jnp.exp(l - w)), kk, lg, lw)
        q["r_bd"] = each(lambda x, l: block_diag(x * jnp.exp(l)), r, lg)
        yield
        q["b_bd"] = each(lambda x, gi: block_diag(x * gi), bvec, g_inv)
        q["k_bd"] = each(lambda x, gi: block_diag(x * gi), kmod, g_inv)
        q["v_bd"] = each(block_diag, v)
        yield
        q["bend_bd"] = each(lambda x, ge: block_diag(x * ge), bvec, g_end)
        q["kend_bd"] = each(lambda x, ge: block_diag(x * ge), kmod, g_end)
        q["rkv"] = (r, kmod, v)

    def main(slots, q):
        sc = each(lambda a, rr, b, k: _dot(rows(a, rr), rows(b, k), _NT),
                  q["a_bd"], q["r_bd"], q["b_bd"], q["k_bd"])
        yield
        l_ab = each(lambda s: jnp.where(strict, s[:2 * cl, :2 * cl], 0.0), sc)
        m_ak = each(lambda s: jnp.where(strict, s[:2 * cl, 2 * cl:], 0.0), sc)
        m_rb = each(lambda s: jnp.where(incl, s[2 * cl:, :2 * cl], 0.0), sc)
        m_rk = each(lambda s: jnp.where(incl, s[2 * cl:, 2 * cl:], 0.0), sc)
        tinv = each(lambda l: eye + l, l_ab)
        lpow = each(lambda p: _dot(p, p, _NN), l_ab)
        yield
        for _ in range(cl.bit_length() - 3):
            z = each(lambda p, t: _dot(rows(p, t), p, _NN), lpow, tinv)
            lpow = each(lambda zz: zz[:2 * cl], z)
            tinv = each(lambda t, zz: t + zz[2 * cl:], tinv, z)
            yield
        tinv = each(lambda t, p: t + _dot(t, p, _NN), tinv, lpow)
        yield
        st = [st_ref[i] for i in slots]
        rhs = each(lambda a, mm, s, vv: _dot(cols(a, mm), rows(s, vv), _NN), q["a_bd"], m_ak, st, q["v_bd"])
        yield
        u = each(lambda t, x: _dot(t, x, _NN), tinv, rhs)
        yield
        y_bd = each(lambda rr, mb, mk, s, uu, vv: _dot(cols(rr, mb, mk), rows(s, uu, vv), _NN),
                    q["r_bd"], m_rb, m_rk, st, u, q["v_bd"])
        yield
        for n, i in enumerate(slots):
            st_ref[i] = q["gcol"][n] * st[n] + _dot(rows(q["bend_bd"][n], q["kend_bd"][n]),
                                                    rows(u[n], q["v_bd"][n]), _TN)
        q["y"] = each(lambda x: x[:cl] + x[cl:], y_bd)

    def tail(grp, q):
        par = lambda ref: [ref[:, pp * pw:(pp + 1) * pw] for _, pp in grp]
        r, kmod, v = q["rkv"]
        y = q["y"]
        mean = each(lambda s: s * inv_n, head_sums(y))
        bon = head_sums(each(lambda a, b, p: a * b * p, r, kmod, par(rk_ref)))
        yield
        dy = each(lambda a, b: a - b, y, mean)
        var = each(lambda s: s * inv_n, head_sums(each(lambda x: x * x, dy)))
        yield
        for n, (b, pp) in enumerate(grp):
            sl = slice(pp * pw, (pp + 1) * pw)
            yn = dy[n] * lax.rsqrt(var[n] + LNX_EPS) * lnw_ref[:, sl] + lnb_ref[:, sl]
            o_ref[b, :, sl] = ((yn + bon[n] * v[n]) * g_ref[b, :, sl]).astype(o_ref.dtype)

    def issue(*phases):
        live = list(phases)
        while live:
            for ph in list(live):
                if next(ph, live) is live:
                    live.remove(ph)

    groups = [[(b, pp) for pp in range(pairs)] for b in range(nb)]
    slots = [[b * pairs + pp for pp in range(pairs)] for b in range(nb)]
    state = [dict() for _ in groups]
    issue(prep(groups[0], state[0]))
    for gi in range(len(groups)):
        phases = [main(slots[gi], state[gi])]
        if gi + 1 < len(groups):
            phases.append(prep(groups[gi + 1], state[gi + 1]))
        if gi > 0:
            phases.append(tail(groups[gi - 1], state[gi - 1]))
        issue(*phases)
    issue(tail(groups[-1], state[-1]))


def _scan(ps, lig, k_k, k_a, r_k, lnx_w, lnx_b, nb, seq, dr):
    pw = 2 * HEAD_DIM
    pairs = 16 if dr % (16 * pw) == 0 else 1
    bw = pairs * pw
    npb = dr // bw
    nc = seq // CHUNK
    ps3 = ps.reshape(nb, seq, ps.shape[1])
    lig3 = lig.reshape(nb, seq, lig.shape[1])
    tok = lambda off: pl.BlockSpec((nb, CHUNK, bw), lambda p, c: (0, c, off * npb + p))
    par = pl.BlockSpec((1, bw), lambda p, c: (0, p))
    o = pl.pallas_call(
        functools.partial(_scan_kernel, nb=nb, pairs=pairs),
        out_shape=jax.ShapeDtypeStruct((nb, seq, dr), BF16),
        grid=(npb, nc),
        in_specs=[tok(0), tok(1), tok(2), tok(0), tok(1), tok(2), par, par, par, par, par],
        out_specs=pl.BlockSpec((nb, CHUNK, bw), lambda p, c: (0, c, p)),
        scratch_shapes=[pltpu.VMEM((nb * pairs, pw, pw), F32)],
        compiler_params=_params("arbitrary", "arbitrary"),
        name="rwkv_scan",
    )(ps3, ps3, ps3, lig3, lig3, lig3, k_k, k_a, r_k, lnx_w, lnx_b)
    return o.reshape(nb * seq, dr)


def _merge_kernel(o_ref, y_ref, wa_ref, wb_ref, sa_ref, sb_ref, out_ref):
    ya = jnp.dot(o_ref[...], wa_ref[...], preferred_element_type=F32)
    yb = jnp.dot(y_ref[...], wb_ref[...], preferred_element_type=F32)
    out_ref[...] = (sa_ref[...].astype(F32) * ya + sb_ref[...].astype(F32) * yb).astype(out_ref.dtype)


def _merge(o, ycb, wa, wb, sg, rows_per_batch):
    m, dr = o.shape
    dc = ycb.shape[1]
    d = wa.shape[1]
    tm = _tile(rows_per_batch, 1024, SUBLANES)
    tn = _tile(d, 1024, LANES)
    nj = d // tn
    return pl.pallas_call(
        _merge_kernel,
        out_shape=jax.ShapeDtypeStruct((m, d), BF16),
        grid=(m // tm, nj),
        in_specs=[pl.BlockSpec((tm, dr), lambda i, j: (i, 0)),
                  pl.BlockSpec((tm, dc), lambda i, j: (i, 0)),
                  pl.BlockSpec((dr, tn), lambda i, j: (0, j)),
                  pl.BlockSpec((dc, tn), lambda i, j: (0, j)),
                  pl.BlockSpec((tm, tn), lambda i, j: (i, j)),
                  pl.BlockSpec((tm, tn), lambda i, j: (i, nj + j))],
        out_specs=pl.BlockSpec((tm, tn), lambda i, j: (i, j)),
        compiler_params=_params("arbitrary", "arbitrary"),
        name="merge",
    )(o, ycb, wa, wb, sg, sg)


def _resid_kernel(a_ref, w_ref, x_ref, gt_ref, o_ref):
    y = jnp.dot(a_ref[...], w_ref[...], preferred_element_type=F32)
    o_ref[...] = x_ref[...] + gt_ref[...] * y


def _resid(a, w, x2d, mod3, gate_idx, rows_per_batch, tm_pref, tn_pref, name, cols_outer):
    m, kd = a.shape
    d = w.shape[1]
    tm = _tile(rows_per_batch, tm_pref, SUBLANES)
    tn = _tile(d, tn_pref, LANES)
    tpb = rows_per_batch // tm
    if cols_outer:
        grid = (d // tn, m // tm)
        ij = lambda f: (lambda j, i: f(i, j))
    else:
        grid = (m // tm, d // tn)
        ij = lambda f: f
    return pl.pallas_call(
        _resid_kernel,
        out_shape=jax.ShapeDtypeStruct((m, d), F32),
        grid=grid,
        in_specs=[pl.BlockSpec((tm, kd), ij(lambda i, j: (i, 0))),
                  pl.BlockSpec((kd, tn), ij(lambda i, j: (0, j))),
                  pl.BlockSpec((tm, tn), ij(lambda i, j: (i, j))),
                  pl.BlockSpec((None, 1, tn), ij(lambda i, j: ((i // tpb) * N_ADA + gate_idx, 0, j)))],
        out_specs=pl.BlockSpec((tm, tn), ij(lambda i, j: (i, j))),
        compiler_params=_params("arbitrary", "arbitrary"),
        name=name,
    )(a, w, x2d, mod3)


def _ffn_up_kernel(h_ref, wg_ref, wv_ref, cw_ref, o_ref, carry_ref, *, tpb, n_sub):
    i, j = pl.program_id(0), pl.program_id(1)
    ts = o_ref.shape[0] // n_sub
    cw = cw_ref[...]

    def finish(s, gate, val, prev8):
        gc = _causal_conv3(gate, prev8, cw)
        o_ref[s * ts:(s + 1) * ts, :] = (gc * jax.nn.sigmoid(gc) * val).astype(o_ref.dtype)

    prev8 = jnp.where((i % tpb) == 0, 0.0, carry_ref[j])
    pending = None
    for s in range(n_sub):
        h = h_ref[s * ts:(s + 1) * ts, :wg_ref.shape[0]]
        gate = jnp.dot(h, wg_ref[...], preferred_element_type=F32)
        val = jnp.dot(h, wv_ref[...], preferred_element_type=F32)
        if pending is not None:
            finish(*pending)
        pending = (s, gate, val, prev8)
        prev8 = gate[ts - SUBLANES:]
    finish(*pending)
    carry_ref[j] = prev8


def _ffn_up(h, w, cw, rows_per_batch):
    m, dp = h.shape
    d = w.shape[0]
    dff = cw.shape[1]
    tm = _tile(rows_per_batch, 2048, SUBLANES)
    tc = _tile(dff, 256, LANES)
    ncb = dff // tc
    return pl.pallas_call(
        functools.partial(_ffn_up_kernel, tpb=rows_per_batch // tm, n_sub=tm // _tile(tm, 128, SUBLANES)),
        out_shape=jax.ShapeDtypeStruct((m, dff), BF16),
        grid=(m // tm, ncb),
        in_specs=[pl.BlockSpec((tm, dp), lambda i, j: (i, 0)),
                  pl.BlockSpec((d, tc), lambda i, j: (0, j)),
                  pl.BlockSpec((d, tc), lambda i, j: (0, ncb + j)),
                  pl.BlockSpec((CONV_WIDTH, tc), lambda i, j: (0, j))],
        out_specs=pl.BlockSpec((tm, tc), lambda i, j: (i, j)),
        scratch_shapes=[pltpu.VMEM((ncb, SUBLANES, tc), F32)],
        compiler_params=_params("arbitrary", "arbitrary"),
        name="ffn_up",
    )(h, w, w, cw)


def kernel(x, c, w_ada, b_ada, norm1_gain, w_in, mu_shift, w0, a0, k_k, k_a, r_k, w_lora_decay,
           w_lora_iclr, w_lora_gate, lnx_w, lnx_b, conv_w_mix, w_o_rwkv, w_o_conv, w_out, norm2_gain,
           w_ffn_up, conv_w_ffn, w_ffn_down, final_gain):
    nb, seq, d = x.shape
    m = nb * seq
    depth = w_ada.shape[0]
    dr = w_o_rwkv.shape[1]
    dc = w_o_conv.shape[1]
    n_dec, n_icl, n_gat = w_lora_decay.shape[1], w_lora_iclr.shape[1], w_lora_gate.shape[1]
    n_lora = n_dec + n_icl + n_gat
    n_shift = 3 * dr + n_lora
    assert seq % CHUNK == 0 and dr % (2 * HEAD_DIM) == 0

    lp = _round_up(n_lora, _tile(dr, 512, LANES))
    assert (3 * dr) % lp == 0 and w_in.shape[2] >= 3 * dr + lp

    x2d = x.reshape(m, d)
    for layer in range(depth):
        wi = w_in[layer].astype(BF16)
        mu_a = jnp.pad(mu_shift[layer], (0, 3 * dr + lp - n_shift)).reshape(1, -1)
        w_c = wi[:, n_shift:n_shift + 3 * dc]
        w_g = wi[:, n_shift + 3 * dc:]
        w_l = jnp.zeros((lp, 3 * dr), F32)
        w_l = w_l.at[:n_dec, :dr].set(w_lora_decay[layer])
        w_l = w_l.at[n_dec:n_dec + n_icl, dr:2 * dr].set(w_lora_iclr[layer])
        w_l = w_l.at[n_dec + n_icl:n_lora, 2 * dr:].set(w_lora_gate[layer]).astype(BF16)
        pv_l = jnp.concatenate([w0[layer], a0[layer], jnp.zeros((dr,), F32)]).reshape(1, 3 * dr)
        row = lambda p: p.reshape(1, dr)

        mod3 = _ada(c, w_ada[layer], b_ada[layer]).reshape(nb * N_ADA, 1, d)
        h = _norm_mod(x2d, norm1_gain[layer], mod3, 0, 1, seq)
        ps = _proj_shift(h, wi, mu_a, seq, 0, 3 * dr, 2048, _tile(dr, 512, LANES), None, "proj_rkv")
        la = _proj_shift(h, wi, mu_a, seq, 3 * dr, lp, 1024, lp, (n_dec, n_icl), "proj_lora")
        ycb = _proj_conv(h, w_c, conv_w_mix[layer], seq)
        sg = _proj_gate(h, w_g, seq)
        lig = _lora(la, w_l, pv_l, seq, lp, 0, dr)
        o = _scan(ps, lig, row(k_k[layer]), row(k_a[layer]), row(r_k[layer]), row(lnx_w[layer]),
                  row(lnx_b[layer]), nb, seq, dr)
        merged = _merge(o, ycb, w_o_rwkv[layer].astype(BF16), w_o_conv[layer].astype(BF16), sg, seq)
        x1 = _resid(merged, w_out[layer].astype(BF16), x2d, mod3, 2, seq, 1024, 1024, "attn_out", False)
        h2 = _norm_mod(x1, norm2_gain[layer], mod3, 3, 4, seq)
        act = _ffn_up(h2, w_ffn_up[layer].astype(BF16), conv_w_ffn[layer], seq)
        x2d = _resid(act, w_ffn_down[layer].astype(BF16), x1, mod3, 5, seq, 512, 512, "ffn_down", True)
    return _final_norm(x2d, final_gain).reshape(nb, seq, d)
```

```python
import functools

import jax
import jax.numpy as jnp
from jax import lax
from jax.experimental import pallas as pl
from jax.experimental.pallas import tpu as pltpu

F32 = jnp.float32
BF16 = jnp.bfloat16

NORM_EPS = 1e-6
LNX_EPS = 64e-5
DECAY_SCALE = 0.6065306597126334
HEAD_DIM = 64
N_ADA = 6
CONV_WIDTH = 3
LANES = 128
SUBLANES = 8
CHUNK = 64
VMEM_LIMIT = 56 * 1024 * 1024
LHS_PAD = LANES

_NN = (((1,), (0,)), ((), ()))
_NT = (((1,), (1,)), ((), ()))
_TN = (((0,), (0,)), ((), ()))


def _round_up(n, m):
    return (n + m - 1) // m * m


def _tile(n, pref, quantum):
    t = min(pref, n) // quantum * quantum
    while t > quantum and n % t:
        t -= quantum
    assert t > 0 and n % t == 0, (n, pref, quantum)
    return t


def _params(*sem):
    return pltpu.CompilerParams(dimension_semantics=sem, vmem_limit_bytes=VMEM_LIMIT)


def _ada_kernel(cb_ref, w_ref, b_ref, o_ref, *, nb, kd, tn):
    rep = tn // LANES

    def body(kc, accs):
        k0 = pl.multiple_of(kc * SUBLANES, SUBLANES)
        w = w_ref[pl.ds(k0, SUBLANES), :]
        new = []
        for b in range(nb):
            cv = cb_ref[b, pl.ds(k0, SUBLANES), :]
            cv = cv * jax.nn.sigmoid(cv)
            new.append(accs[b] + w * jnp.concatenate([cv] * rep, axis=1))
        return tuple(new)

    init = tuple(jnp.zeros((SUBLANES, tn), F32) for _ in range(nb))
    accs = lax.fori_loop(0, kd // SUBLANES, body, init, unroll=8)
    rows = [jnp.sum(a, axis=0, keepdims=True) for a in accs]
    o_ref[...] = jnp.concatenate(rows, axis=0) + b_ref[...]


def _ada(c, w, b):
    nb, kd = c.shape
    n = w.shape[1]
    tn = _tile(n, 1024, LANES)
    cb = jnp.broadcast_to(c[:, :, None], (nb, kd, LANES))
    return pl.pallas_call(
        functools.partial(_ada_kernel, nb=nb, kd=kd, tn=tn),
        out_shape=jax.ShapeDtypeStruct((nb, n), F32),
        grid=(n // tn,),
        in_specs=[pl.BlockSpec((nb, kd, LANES), lambda j: (0, 0, 0)),
                  pl.BlockSpec((kd, tn), lambda j: (0, j)),
                  pl.BlockSpec((1, tn), lambda j: (0, j))],
        out_specs=pl.BlockSpec((nb, tn), lambda j: (0, j)),
        compiler_params=_params("arbitrary"),
        name="ada",
    )(cb, w, b.reshape(1, n))


def _cast_kernel(x_ref, o_ref):
    o_ref[...] = x_ref[...].astype(o_ref.dtype)


def _to_bf16(w):
    k, n = w.shape
    tk = _tile(k, 512, 2 * SUBLANES)
    tn = min(2048, _round_up(n, LANES))
    return pl.pallas_call(
        _cast_kernel,
        out_shape=jax.ShapeDtypeStruct((k, n), BF16),
        grid=(k // tk, pl.cdiv(n, tn)),
        in_specs=[pl.BlockSpec((tk, tn), lambda i, j: (i, j))],
        out_specs=pl.BlockSpec((tk, tn), lambda i, j: (i, j)),
        compiler_params=_params("arbitrary", "arbitrary"),
        name="to_bf16",
    )(w)


def _rms(x):
    return x * lax.rsqrt(jnp.mean(x * x, axis=-1, keepdims=True) + NORM_EPS)


def _norm_mod_kernel(x_ref, g_ref, sh_ref, sc_ref, o_ref):
    y = _rms(x_ref[...]) * g_ref[...]
    d = x_ref.shape[1]
    o_ref[:, :d] = (y * (1.0 + sc_ref[...]) + sh_ref[...]).astype(o_ref.dtype)
    o_ref[:, d:] = jnp.zeros((o_ref.shape[0], o_ref.shape[1] - d), o_ref.dtype)


def _norm_mod(x2d, gain, mod3, shift_idx, scale_idx, rows_per_batch):
    m, d = x2d.shape
    tm = _tile(rows_per_batch, 512, SUBLANES)
    tpb = rows_per_batch // tm
    return pl.pallas_call(
        _norm_mod_kernel,
        out_shape=jax.ShapeDtypeStruct((m, d + LHS_PAD), BF16),
        grid=(m // tm,),
        in_specs=[pl.BlockSpec((tm, d), lambda i: (i, 0)),
                  pl.BlockSpec((1, d), lambda i: (0, 0)),
                  pl.BlockSpec((None, 1, d), lambda i: ((i // tpb) * N_ADA + shift_idx, 0, 0)),
                  pl.BlockSpec((None, 1, d), lambda i: ((i // tpb) * N_ADA + scale_idx, 0, 0))],
        out_specs=pl.BlockSpec((tm, d + LHS_PAD), lambda i: (i, 0)),
        compiler_params=_params("arbitrary"),
        name="norm_mod",
    )(x2d, gain.reshape(1, d), mod3, mod3)


def _final_norm_kernel(x_ref, g_ref, o_ref):
    o_ref[...] = _rms(x_ref[...]) * g_ref[...]


def _final_norm(x2d, gain):
    m, d = x2d.shape
    tm = _tile(m, 512, SUBLANES)
    return pl.pallas_call(
        _final_norm_kernel,
        out_shape=jax.ShapeDtypeStruct((m, d), F32),
        grid=(m // tm,),
        in_specs=[pl.BlockSpec((tm, d), lambda i: (i, 0)),
                  pl.BlockSpec((1, d), lambda i: (0, 0))],
        out_specs=pl.BlockSpec((tm, d), lambda i: (i, 0)),
        compiler_params=_params("arbitrary"),
        name="final_norm",
    )(x2d, gain.reshape(1, d))


def _shift_rows(p, prev8, s):
    rolled = pltpu.roll(p, s, 0)
    row = lax.broadcasted_iota(jnp.int32, prev8.shape, 0)
    head = jnp.where(row < s, pltpu.roll(prev8, s, 0), rolled[:SUBLANES])
    return jnp.concatenate([head, rolled[SUBLANES:]], axis=0)


def _carry_swap(carry_ref, j, tail, first):
    prev8 = jnp.where(first, 0.0, carry_ref[j])
    carry_ref[j] = tail
    return prev8


def _causal_conv3(z, prev8, cw):
    return cw[0:1] * _shift_rows(z, prev8, 2) + cw[1:2] * _shift_rows(z, prev8, 1) + cw[2:3] * z


def _proj_shift_kernel(h_ref, w_ref, mu_ref, o_ref, carry_ref, *, tpb, n_sub, lora_split):
    i, j = pl.program_id(0), pl.program_id(1)
    tn = o_ref.shape[1]
    ts = o_ref.shape[0] // n_sub
    mu = mu_ref[...]
    col = lax.broadcasted_iota(jnp.int32, (1, tn), 1)

    def finish(s, p, prev8):
        out = p + (_shift_rows(p, prev8, 1) - p) * mu
        if lora_split is not None:
            n_tanh, n_lin = lora_split
            out = jnp.where(col < n_tanh, jnp.tanh(out),
                            jnp.where(col < n_tanh + n_lin, out, jax.nn.sigmoid(out)))
        o_ref[s * ts:(s + 1) * ts, :] = out

    prev8 = jnp.where((i % tpb) == 0, 0.0, carry_ref[j])
    pending = None
    for s in range(n_sub):
        p = jnp.dot(h_ref[s * ts:(s + 1) * ts, :w_ref.shape[0]], w_ref[...], preferred_element_type=F32)
        if pending is not None:
            finish(*pending)
        pending = (s, p, prev8)
        prev8 = p[ts - SUBLANES:]
    finish(*pending)
    carry_ref[j] = prev8


def _proj_shift(h, w, mu, rows_per_batch, col0, n, tm_pref, tn, lora_split, name):
    m, dp = h.shape
    d = w.shape[0]
    tm = _tile(rows_per_batch, tm_pref, SUBLANES)
    j0 = col0 // tn
    return pl.pallas_call(
        functools.partial(_proj_shift_kernel, tpb=rows_per_batch // tm, n_sub=tm // _tile(tm, 256, SUBLANES),
                          lora_split=lora_split),
        out_shape=jax.ShapeDtypeStruct((m, n), F32),
        grid=(m // tm, n // tn),
        in_specs=[pl.BlockSpec((tm, dp), lambda i, j: (i, 0)),
                  pl.BlockSpec((d, tn), lambda i, j: (0, j0 + j)),
                  pl.BlockSpec((1, tn), lambda i, j: (0, j0 + j))],
        out_specs=pl.BlockSpec((tm, tn), lambda i, j: (i, j)),
        scratch_shapes=[pltpu.VMEM((n // tn, SUBLANES, tn), F32)],
        compiler_params=_params("arbitrary", "arbitrary"),
        name=name,
    )(h, w, mu)


def _proj_conv_kernel(h_ref, wb_ref, wc_ref, wx_ref, cw_ref, o_ref, carry_ref, *, tpb):
    i, j = pl.program_id(0), pl.program_id(1)
    tm = o_ref.shape[0]
    h = h_ref[:, :wb_ref.shape[0]]
    z = (jnp.dot(h, wc_ref[...], preferred_element_type=F32)
         * jnp.dot(h, wx_ref[...], preferred_element_type=F32))
    prev8 = _carry_swap(carry_ref, j, z[tm - SUBLANES:], (i % tpb) == 0)
    conv = _causal_conv3(z, prev8, cw_ref[...])
    o_ref[...] = (jnp.dot(h, wb_ref[...], preferred_element_type=F32) * conv).astype(o_ref.dtype)


def _proj_conv(h, w, cw, rows_per_batch):
    m, dp = h.shape
    d = w.shape[0]
    dc = cw.shape[1]
    tm = _tile(rows_per_batch, 1024, SUBLANES)
    tc = _tile(dc, 512, LANES)
    ncb = dc // tc
    wspec = lambda grp: pl.BlockSpec((d, tc), lambda i, j: (0, grp * ncb + j))
    return pl.pallas_call(
        functools.partial(_proj_conv_kernel, tpb=rows_per_batch // tm),
        out_shape=jax.ShapeDtypeStruct((m, dc), BF16),
        grid=(m // tm, ncb),
        in_specs=[pl.BlockSpec((tm, dp), lambda i, j: (i, 0)), wspec(0), wspec(1), wspec(2),
                  pl.BlockSpec((CONV_WIDTH, tc), lambda i, j: (0, j))],
        out_specs=pl.BlockSpec((tm, tc), lambda i, j: (i, j)),
        scratch_shapes=[pltpu.VMEM((ncb, SUBLANES, tc), F32)],
        compiler_params=_params("arbitrary", "arbitrary"),
        name="proj_conv",
    )(h, w, w, w, cw)


def _proj_gate_kernel(h_ref, w_ref, o_ref, *, n_sub):
    ts = o_ref.shape[0] // n_sub
    for s in range(n_sub):
        p = jnp.dot(h_ref[s * ts:(s + 1) * ts, :w_ref.shape[0]], w_ref[...], preferred_element_type=F32)
        o_ref[s * ts:(s + 1) * ts, :] = jax.nn.sigmoid(p).astype(o_ref.dtype)


def _proj_gate(h, w, rows_per_batch):
    m, dp = h.shape
    d = w.shape[0]
    n = w.shape[1]
    tm = _tile(rows_per_batch, 2048, SUBLANES)
    tn = _tile(n, 512, LANES)
    return pl.pallas_call(
        functools.partial(_proj_gate_kernel, n_sub=tm // _tile(tm, 256, SUBLANES)),
        out_shape=jax.ShapeDtypeStruct((m, n), BF16),
        grid=(m // tm, n // tn),
        in_specs=[pl.BlockSpec((tm, dp), lambda i, j: (i, 0)),
                  pl.BlockSpec((d, tn), lambda i, j: (0, j))],
        out_specs=pl.BlockSpec((tm, tn), lambda i, j: (i, j)),
        compiler_params=_params("arbitrary", "arbitrary"),
        name="proj_gate",
    )(h, w)


def _sigmoid_tanh(x):
    return 0.5 * jnp.tanh(0.5 * x) + 0.5


def _lora_kernel(a_ref, w_ref, pv_ref, o_ref, *, tiles_per_group):
    grp = pl.program_id(1) // tiles_per_group
    pre = jnp.dot(a_ref[...].astype(BF16), w_ref[...], preferred_element_type=F32) + pv_ref[...]

    @pl.when(grp == 0)
    def _():
        o_ref[...] = -DECAY_SCALE * _sigmoid_tanh(pre)

    @pl.when(grp == 1)
    def _():
        o_ref[...] = _sigmoid_tanh(pre)

    @pl.when(grp == 2)
    def _():
        o_ref[...] = pre


def _lora(ps, w, pv, rows_per_batch, lp, col_block, dr):
    m = ps.shape[0]
    n = w.shape[1]
    tm = _tile(rows_per_batch, 1024, SUBLANES)
    tn = _tile(dr, 2048, LANES)
    return pl.pallas_call(
        functools.partial(_lora_kernel, tiles_per_group=dr // tn),
        out_shape=jax.ShapeDtypeStruct((m, n), F32),
        grid=(m // tm, n // tn),
        in_specs=[pl.BlockSpec((tm, lp), lambda i, j: (i, col_block)),
                  pl.BlockSpec((lp, tn), lambda i, j: (0, j)),
                  pl.BlockSpec((1, tn), lambda i, j: (0, j))],
        out_specs=pl.BlockSpec((tm, tn), lambda i, j: (i, j)),
        compiler_params=_params("arbitrary", "arbitrary"),
        name="lora",
    )(ps, w, pv)


def _split(x, n):
    pieces = []
    for _ in range(n):
        p = x.astype(BF16)
        pieces.append(p)
        x = x - p.astype(F32)
    return pieces


def _dot(a, b, dims, na=1, nb=1):
    pa = [a] if a.dtype == BF16 else _split(a, na)
    pb = [b] if b.dtype == BF16 else _split(b, nb)
    depth = max(len(pa), len(pb))
    acc = None
    for ia, xa in enumerate(pa):
        for ib, xb in enumerate(pb):
            if ia + ib >= depth:
                continue
            d = lax.dot_general(xa, xb, dims, preferred_element_type=F32)
            acc = d if acc is None else acc + d
    return acc


def _scan_kernel(r_ref, k_ref, v_ref, lw_ref, ic_ref, g_ref, kk_ref, ka_ref, rk_ref, lnw_ref, lnb_ref,
                 o_ref, st_ref, *, nb, pairs):
    cl = CHUNK
    pw = 2 * HEAD_DIM

    @pl.when(pl.program_id(1) == 0)
    def _():
        st_ref[...] = jnp.zeros_like(st_ref)

    ri = lax.broadcasted_iota(jnp.int32, (2 * cl, 2 * cl), 0)
    ci = lax.broadcasted_iota(jnp.int32, (2 * cl, 2 * cl), 1)
    strict = ri > ci
    incl = ri >= ci
    eye = jnp.where(ri == ci, 1.0, 0.0).astype(F32)
    hi = lax.broadcasted_iota(jnp.int32, (pw, pw), 0) // HEAD_DIM
    hj = lax.broadcasted_iota(jnp.int32, (pw, pw), 1) // HEAD_DIM
    head_ones = jnp.where(hi == hj, 1.0, 0.0).astype(BF16)
    ti = lax.broadcasted_iota(jnp.int32, (cl, cl), 0)
    tj = lax.broadcasted_iota(jnp.int32, (cl, cl), 1)
    cum_ones = jnp.where(ti >= tj, 1.0, 0.0).astype(BF16)
    first_head = lax.broadcasted_iota(jnp.int32, (1, pw), 1) < HEAD_DIM
    inv_n = 1.0 / HEAD_DIM

    def block_diag(x):
        return jnp.concatenate([jnp.where(first_head, x, 0.0), jnp.where(first_head, 0.0, x)], axis=0)

    def head_sums(xs):
        s = _dot(jnp.concatenate(xs, axis=0), head_ones, _NN)
        return [s[i * cl:(i + 1) * cl] for i in range(len(xs))]

    def each(f, *ls):
        return [f(*xs) for xs in zip(*ls)]

    def rows(*xs):
        return jnp.concatenate(xs, axis=0)

    def cols(*xs):
        return jnp.concatenate(xs, axis=1)

    def prep(grp, q):
        tok = lambda ref: [ref[b, :, pp * pw:(pp + 1) * pw] for b, pp in grp]
        par = lambda ref: [ref[:, pp * pw:(pp + 1) * pw] for _, pp in grp]
        r, kraw, v, lw, ic = tok(r_ref), tok(k_ref), tok(v_ref), tok(lw_ref), tok(ic_ref)
        kk = each(lambda k, p: k * p, kraw, par(kk_ref))
        ss = head_sums(each(lambda x: x * x, kk))
        lg = each(lambda w: _dot(cum_ones, w, _NN, nb=2), lw)
        yield
        kk = each(lambda x, s: x / jnp.maximum(jnp.sqrt(s), 1e-12), kk, ss)
        kmod = each(lambda k, i, p: k * (1.0 + (i - 1.0) * p), kraw, ic, par(ka_ref))
        bvec = each(lambda x, i: x * i, kk, ic)
        yield
        g_inv = each(lambda x: jnp.exp(-x), lg)
        g_end = each(lambda x: jnp.exp(x[cl - 1:cl, :] - x), lg)
        q["gcol"] = each(lambda x: jnp.transpose(jnp.broadcast_to(jnp.exp(x[cl - 1:cl, :]), (pw, pw))), lg)
        yield
        q["a_bd"] = each(lambda x, l, w: block_diag(-x * jnp.exp(l - w)), kk, lg, lw)
        q["r_bd"] = each(lambda x, l: block_diag(x * jnp.exp(l)), r, lg)
        yield
        q["b_bd"] = each(lambda x, gi: block_diag(x * gi), bvec, g_inv)
        q["k_bd"] = each(lambda x, gi: block_diag(x * gi), kmod, g_inv)
        q["v_bd"] = each(block_diag, v)
        yield
        q["bend_bd"] = each(lambda x, ge: block_diag(x * ge), bvec, g_end)
        q["kend_bd"] = each(lambda x, ge: block_diag(x * ge), kmod, g_end)
        q["rkv"] = (r, kmod, v)

    def main(slots, q):
        sc = each(lambda a, rr, b, k: _dot(rows(a, rr), rows(b, k), _NT),
                  q["a_bd"], q["r_bd"], q["b_bd"], q["k_bd"])
        yield
        l_ab = each(lambda s: jnp.where(strict, s[:2 * cl, :2 * cl], 0.0), sc)
        m_ak = each(lambda s: jnp.where(strict, s[:2 * cl, 2 * cl:], 0.0), sc)
        m_rb = each(lambda s: jnp.where(incl, s[2 * cl:, :2 * cl], 0.0), sc)
        m_rk = each(lambda s: jnp.where(incl, s[2 * cl:, 2 * cl:], 0.0), sc)
        tinv = each(lambda l: eye + l, l_ab)
        lpow = each(lambda p: _dot(p, p, _NN), l_ab)
        yield
        for _ in range(cl.bit_length() - 3):
            z = each(lambda p, t: _dot(rows(p, t), p, _NN), lpow, tinv)
            lpow = each(lambda zz: zz[:2 * cl], z)
            tinv = each(lambda t, zz: t + zz[2 * cl:], tinv, z)
            yield
        tinv = each(lambda t, p: t + _dot(t, p, _NN), tinv, lpow)
        yield
        st = [st_ref[i] for i in slots]
        rhs = each(lambda a, mm, s, vv: _dot(cols(a, mm), rows(s, vv), _NN), q["a_bd"], m_ak, st, q["v_bd"])
        yield
        u = each(lambda t, x: _dot(t, x, _NN), tinv, rhs)
        yield
        y_bd = each(lambda rr, mb, mk, s, uu, vv: _dot(cols(rr, mb, mk), rows(s, uu, vv), _NN),
                    q["r_bd"], m_rb, m_rk, st, u, q["v_bd"])
        yield
        for n, i in enumerate(slots):
            st_ref[i] = q["gcol"][n] * st[n] + _dot(rows(q["bend_bd"][n], q["kend_bd"][n]),
                                                    rows(u[n], q["v_bd"][n]), _TN)
        q["y"] = each(lambda x: x[:cl] + x[cl:], y_bd)

    def tail(grp, q):
        par = lambda ref: [ref[:, pp * pw:(pp + 1) * pw] for _, pp in grp]
        r, kmod, v = q["rkv"]
        y = q["y"]
        mean = each(lambda s: s * inv_n, head_sums(y))
        bon = head_sums(each(lambda a, b, p: a * b * p, r, kmod, par(rk_ref)))
        yield
        dy = each(lambda a, b: a - b, y, mean)
        var = each(lambda s: s * inv_n, head_sums(each(lambda x: x * x, dy)))
        yield
        for n, (b, pp) in enumerate(grp):
            sl = slice(pp * pw, (pp + 1) * pw)
            yn = dy[n] * lax.rsqrt(var[n] + LNX_EPS) * lnw_ref[:, sl] + lnb_ref[:, sl]
            o_ref[b, :, sl] = ((yn + bon[n] * v[n]) * g_ref[b, :, sl]).astype(o_ref.dtype)

    def issue(*phases):
        live = list(phases)
        while live:
            for ph in list(live):
                if next(ph, live) is live:
                    live.remove(ph)

    groups = [[(b, pp) for pp in range(pairs)] for b in range(nb)]
    slots = [[b * pairs + pp for pp in range(pairs)] for b in range(nb)]
    state = [dict() for _ in groups]
    issue(prep(groups[0], state[0]))
    for gi in range(len(groups)):
        phases = [main(slots[gi], state[gi])]
        if gi + 1 < len(groups):
            phases.append(prep(groups[gi + 1], state[gi + 1]))
        if gi > 0:
            phases.append(tail(groups[gi - 1], state[gi - 1]))
        issue(*phases)
    issue(tail(groups[-1], state[-1]))


def _scan(ps, lig, k_k, k_a, r_k, lnx_w, lnx_b, nb, seq, dr):
    pw = 2 * HEAD_DIM
    pairs = 16 if dr % (16 * pw) == 0 else 1
    bw = pairs * pw
    npb = dr // bw
    nc = seq // CHUNK
    ps3 = ps.reshape(nb, seq, ps.shape[1])
    lig3 = lig.reshape(nb, seq, lig.shape[1])
    tok = lambda off: pl.BlockSpec((nb, CHUNK, bw), lambda p, c: (0, c, off * npb + p))
    par = pl.BlockSpec((1, bw), lambda p, c: (0, p))
    o = pl.pallas_call(
        functools.partial(_scan_kernel, nb=nb, pairs=pairs),
        out_shape=jax.ShapeDtypeStruct((nb, seq, dr), BF16),
        grid=(npb, nc),
        in_specs=[tok(0), tok(1), tok(2), tok(0), tok(1), tok(2), par, par, par, par, par],
        out_specs=pl.BlockSpec((nb, CHUNK, bw), lambda p, c: (0, c, p)),
        scratch_shapes=[pltpu.VMEM((nb * pairs, pw, pw), F32)],
        compiler_params=_params("arbitrary", "arbitrary"),
        name="rwkv_scan",
    )(ps3, ps3, ps3, lig3, lig3, lig3, k_k, k_a, r_k, lnx_w, lnx_b)
    return o.reshape(nb * seq, dr)


def _merge_kernel(o_ref, y_ref, wa_ref, wb_ref, sa_ref, sb_ref, out_ref):
    ya = jnp.dot(o_ref[...], wa_ref[...], preferred_element_type=F32)
    yb = jnp.dot(y_ref[...], wb_ref[...], preferred_element_type=F32)
    out_ref[...] = (sa_ref[...].astype(F32) * ya + sb_ref[...].astype(F32) * yb).astype(out_ref.dtype)


def _merge(o, ycb, wa, wb, sg, rows_per_batch):
    m, dr = o.shape
    dc = ycb.shape[1]
    d = wa.shape[1]
    tm = _tile(rows_per_batch, 1024, SUBLANES)
    tn = _tile(d, 1024, LANES)
    nj = d // tn
    return pl.pallas_call(
        _merge_kernel,
        out_shape=jax.ShapeDtypeStruct((m, d), BF16),
        grid=(m // tm, nj),
        in_specs=[pl.BlockSpec((tm, dr), lambda i, j: (i, 0)),
                  pl.BlockSpec((tm, dc), lambda i, j: (i, 0)),
                  pl.BlockSpec((dr, tn), lambda i, j: (0, j)),
                  pl.BlockSpec((dc, tn), lambda i, j: (0, j)),
                  pl.BlockSpec((tm, tn), lambda i, j: (i, j)),
                  pl.BlockSpec((tm, tn), lambda i, j: (i, nj + j))],
        out_specs=pl.BlockSpec((tm, tn), lambda i, j: (i, j)),
        compiler_params=_params("arbitrary", "arbitrary"),
        name="merge",
    )(o, ycb, wa, wb, sg, sg)


def _resid_kernel(a_ref, w_ref, x_ref, gt_ref, o_ref):
    y = jnp.dot(a_ref[...], w_ref[...], preferred_element_type=F32)
    o_ref[...] = x_ref[...] + gt_ref[...] * y


def _resid(a, w, x2d, mod3, gate_idx, rows_per_batch, tm_pref, tn_pref, name, cols_outer):
    m, kd = a.shape
    d = w.shape[1]
    tm = _tile(rows_per_batch, tm_pref, SUBLANES)
    tn = _tile(d, tn_pref, LANES)
    tpb = rows_per_batch // tm
    if cols_outer:
        grid = (d // tn, m // tm)
        ij = lambda f: (lambda j, i: f(i, j))
    else:
        grid = (m // tm, d // tn)
        ij = lambda f: f
    return pl.pallas_call(
        _resid_kernel,
        out_shape=jax.ShapeDtypeStruct((m, d), F32),
        grid=grid,
        in_specs=[pl.BlockSpec((tm, kd), ij(lambda i, j: (i, 0))),
                  pl.BlockSpec((kd, tn), ij(lambda i, j: (0, j))),
                  pl.BlockSpec((tm, tn), ij(lambda i, j: (i, j))),
                  pl.BlockSpec((None, 1, tn), ij(lambda i, j: ((i // tpb) * N_ADA + gate_idx, 0, j)))],
        out_specs=pl.BlockSpec((tm, tn), ij(lambda i, j: (i, j))),
        compiler_params=_params("arbitrary", "arbitrary"),
        name=name,
    )(a, w, x2d, mod3)


def _ffn_up_kernel(h_ref, wg_ref, wv_ref, cw_ref, o_ref, carry_ref, *, tpb, n_sub):
    i, j = pl.program_id(0), pl.program_id(1)
    ts = o_ref.shape[0] // n_sub
    cw = cw_ref[...]

    def finish(s, gate, val, prev8):
        gc = _causal_conv3(gate, prev8, cw)
        o_ref[s * ts:(s + 1) * ts, :] = (gc * jax.nn.sigmoid(gc) * val).astype(o_ref.dtype)

    prev8 = jnp.where((i % tpb) == 0, 0.0, carry_ref[j])
    pending = None
    for s in range(n_sub):
        h = h_ref[s * ts:(s + 1) * ts, :wg_ref.shape[0]]
        gate = jnp.dot(h, wg_ref[...], preferred_element_type=F32)
        val = jnp.dot(h, wv_ref[...], preferred_element_type=F32)
        if pending is not None:
            finish(*pending)
        pending = (s, gate, val, prev8)
        prev8 = gate[ts - SUBLANES:]
    finish(*pending)
    carry_ref[j] = prev8


def _ffn_up(h, w, cw, rows_per_batch):
    m, dp = h.shape
    d = w.shape[0]
    dff = cw.shape[1]
    tm = _tile(rows_per_batch, 2048, SUBLANES)
    tc = _tile(dff, 256, LANES)
    ncb = dff // tc
    return pl.pallas_call(
        functools.partial(_ffn_up_kernel, tpb=rows_per_batch // tm, n_sub=tm // _tile(tm, 128, SUBLANES)),
        out_shape=jax.ShapeDtypeStruct((m, dff), BF16),
        grid=(m // tm, ncb),
        in_specs=[pl.BlockSpec((tm, dp), lambda i, j: (i, 0)),
                  pl.BlockSpec((d, tc), lambda i, j: (0, j)),
                  pl.BlockSpec((d, tc), lambda i, j: (0, ncb + j)),
                  pl.BlockSpec((CONV_WIDTH, tc), lambda i, j: (0, j))],
        out_specs=pl.BlockSpec((tm, tc), lambda i, j: (i, j)),
        scratch_shapes=[pltpu.VMEM((ncb, SUBLANES, tc), F32)],
        compiler_params=_params("arbitrary", "arbitrary"),
        name="ffn_up",
    )(h, w, w, cw)


def kernel(x, c, w_ada, b_ada, norm1_gain, w_in, mu_shift, w0, a0, k_k, k_a, r_k, w_lora_decay,
           w_lora_iclr, w_lora_gate, lnx_w, lnx_b, conv_w_mix, w_o_rwkv, w_o_conv, w_out, norm2_gain,
           w_ffn_up, conv_w_ffn, w_ffn_down, final_gain):
    nb, seq, d = x.shape
    m = nb * seq
    depth = w_ada.shape[0]
    dr = w_o_rwkv.shape[1]
    dc = w_o_conv.shape[1]
    n_dec, n_icl, n_gat = w_lora_decay.shape[1], w_lora_iclr.shape[1], w_lora_gate.shape[1]
    n_lora = n_dec + n_icl + n_gat
    n_shift = 3 * dr + n_lora
    assert seq % CHUNK == 0 and dr % (2 * HEAD_DIM) == 0

    lp = _round_up(n_lora, _tile(dr, 512, LANES))
    assert (3 * dr) % lp == 0 and w_in.shape[2] >= 3 * dr + lp

    x2d = x.reshape(m, d)
    for layer in range(depth):
        wi = _to_bf16(w_in[layer])
        mu_a = jnp.pad(mu_shift[layer], (0, 3 * dr + lp - n_shift)).reshape(1, -1)
        w_c = wi[:, n_shift:n_shift + 3 * dc]
        w_g = wi[:, n_shift + 3 * dc:]
        w_l = jnp.zeros((lp, 3 * dr), F32)
        w_l = w_l.at[:n_dec, :dr].set(w_lora_decay[layer])
        w_l = w_l.at[n_dec:n_dec + n_icl, dr:2 * dr].set(w_lora_iclr[layer])
        w_l = w_l.at[n_dec + n_icl:n_lora, 2 * dr:].set(w_lora_gate[layer]).astype(BF16)
        pv_l = jnp.concatenate([w0[layer], a0[layer], jnp.zeros((dr,), F32)]).reshape(1, 3 * dr)
        row = lambda p: p.reshape(1, dr)

        mod3 = _ada(c, w_ada[layer], b_ada[layer]).reshape(nb * N_ADA, 1, d)
        h = _norm_mod(x2d, norm1_gain[layer], mod3, 0, 1, seq)
        ps = _proj_shift(h, wi, mu_a, seq, 0, 3 * dr, 2048, _tile(dr, 512, LANES), None, "proj_rkv")
        la = _proj_shift(h, wi, mu_a, seq, 3 * dr, lp, 1024, lp, (n_dec, n_icl), "proj_lora")
        ycb = _proj_conv(h, w_c, conv_w_mix[layer], seq)
        sg = _proj_gate(h, w_g, seq)
        lig = _lora(la, w_l, pv_l, seq, lp, 0, dr)
        o = _scan(ps, lig, row(k_k[layer]), row(k_a[layer]), row(r_k[layer]), row(lnx_w[layer]),
                  row(lnx_b[layer]), nb, seq, dr)
        merged = _merge(o, ycb, w_o_rwkv[layer].astype(BF16), w_o_conv[layer].astype(BF16), sg, seq)
        x1 = _resid(merged, w_out[layer].astype(BF16), x2d, mod3, 2, seq, 1024, 1024, "attn_out", False)
        h2 = _norm_mod(x1, norm2_gain[layer], mod3, 3, 4, seq)
        act = _ffn_up(h2, w_ffn_up[layer].astype(BF16), conv_w_ffn[layer], seq)
        x2d = _resid(act, w_ffn_down[layer].astype(BF16), x1, mod3, 5, seq, 512, 512, "ffn_down", True)
    return _final_norm(x2d, final_gain).reshape(nb, seq, d)
```

```python
import functools

import jax
import jax.numpy as jnp
from jax import lax
from jax.experimental import pallas as pl
from jax.experimental.pallas import tpu as pltpu

F32 = jnp.float32
BF16 = jnp.bfloat16

NORM_EPS = 1e-6
LNX_EPS = 64e-5
DECAY_SCALE = 0.6065306597126334
HEAD_DIM = 64
N_ADA = 6
CONV_WIDTH = 3
LANES = 128
SUBLANES = 8
CHUNK = 64
VMEM_LIMIT = 56 * 1024 * 1024
LHS_PAD = LANES

_NN = (((1,), (0,)), ((), ()))
_NT = (((1,), (1,)), ((), ()))
_TN = (((0,), (0,)), ((), ()))


def _round_up(n, m):
    return (n + m - 1) // m * m


def _tile(n, pref, quantum):
    t = min(pref, n) // quantum * quantum
    while t > quantum and n % t:
        t -= quantum
    assert t > 0 and n % t == 0, (n, pref, quantum)
    return t


def _params(*sem):
    return pltpu.CompilerParams(dimension_semantics=sem, vmem_limit_bytes=VMEM_LIMIT)


def _ada_kernel(cb_ref, w_ref, b_ref, o_ref, *, nb, kd, tn):
    rep = tn // LANES

    def body(kc, accs):
        k0 = pl.multiple_of(kc * SUBLANES, SUBLANES)
        w = w_ref[pl.ds(k0, SUBLANES), :]
        new = []
        for b in range(nb):
            cv = cb_ref[b, pl.ds(k0, SUBLANES), :]
            cv = cv * jax.nn.sigmoid(cv)
            new.append(accs[b] + w * jnp.concatenate([cv] * rep, axis=1))
        return tuple(new)

    init = tuple(jnp.zeros((SUBLANES, tn), F32) for _ in range(nb))
    accs = lax.fori_loop(0, kd // SUBLANES, body, init, unroll=8)
    rows = [jnp.sum(a, axis=0, keepdims=True) for a in accs]
    o_ref[...] = jnp.concatenate(rows, axis=0) + b_ref[...]


def _ada(c, w, b):
    nb, kd = c.shape
    n = w.shape[1]
    tn = _tile(n, 1024, LANES)
    cb = jnp.broadcast_to(c[:, :, None], (nb, kd, LANES))
    return pl.pallas_call(
        functools.partial(_ada_kernel, nb=nb, kd=kd, tn=tn),
        out_shape=jax.ShapeDtypeStruct((nb, n), F32),
        grid=(n // tn,),
        in_specs=[pl.BlockSpec((nb, kd, LANES), lambda j: (0, 0, 0)),
                  pl.BlockSpec((kd, tn), lambda j: (0, j)),
                  pl.BlockSpec((1, tn), lambda j: (0, j))],
        out_specs=pl.BlockSpec((nb, tn), lambda j: (0, j)),
        compiler_params=_params("arbitrary"),
        name="ada",
    )(cb, w, b.reshape(1, n))


def _rms(x):
    return x * lax.rsqrt(jnp.mean(x * x, axis=-1, keepdims=True) + NORM_EPS)


def _norm_mod_kernel(x_ref, g_ref, sh_ref, sc_ref, o_ref):
    y = _rms(x_ref[...]) * g_ref[...]
    d = x_ref.shape[1]
    o_ref[:, :d] = (y * (1.0 + sc_ref[...]) + sh_ref[...]).astype(o_ref.dtype)
    o_ref[:, d:] = jnp.zeros((o_ref.shape[0], o_ref.shape[1] - d), o_ref.dtype)


def _norm_mod(x2d, gain, mod3, shift_idx, scale_idx, rows_per_batch):
    m, d = x2d.shape
    tm = _tile(rows_per_batch, 512, SUBLANES)
    tpb = rows_per_batch // tm
    return pl.pallas_call(
        _norm_mod_kernel,
        out_shape=jax.ShapeDtypeStruct((m, d + LHS_PAD), BF16),
        grid=(m // tm,),
        in_specs=[pl.BlockSpec((tm, d), lambda i: (i, 0)),
                  pl.BlockSpec((1, d), lambda i: (0, 0)),
                  pl.BlockSpec((None, 1, d), lambda i: ((i // tpb) * N_ADA + shift_idx, 0, 0)),
                  pl.BlockSpec((None, 1, d), lambda i: ((i // tpb) * N_ADA + scale_idx, 0, 0))],
        out_specs=pl.BlockSpec((tm, d + LHS_PAD), lambda i: (i, 0)),
        compiler_params=_params("arbitrary"),
        name="norm_mod",
    )(x2d, gain.reshape(1, d), mod3, mod3)


def _final_norm_kernel(x_ref, g_ref, o_ref):
    o_ref[...] = _rms(x_ref[...]) * g_ref[...]


def _final_norm(x2d, gain):
    m, d = x2d.shape
    tm = _tile(m, 512, SUBLANES)
    return pl.pallas_call(
        _final_norm_kernel,
        out_shape=jax.ShapeDtypeStruct((m, d), F32),
        grid=(m // tm,),
        in_specs=[pl.BlockSpec((tm, d), lambda i: (i, 0)),
                  pl.BlockSpec((1, d), lambda i: (0, 0))],
        out_specs=pl.BlockSpec((tm, d), lambda i: (i, 0)),
        compiler_params=_params("arbitrary"),
        name="final_norm",
    )(x2d, gain.reshape(1, d))


def _shift_rows(p, prev8, s):
    rolled = pltpu.roll(p, s, 0)
    row = lax.broadcasted_iota(jnp.int32, prev8.shape, 0)
    head = jnp.where(row < s, pltpu.roll(prev8, s, 0), rolled[:SUBLANES])
    return jnp.concatenate([head, rolled[SUBLANES:]], axis=0)


def _carry_swap(carry_ref, j, tail, first):
    prev8 = jnp.where(first, 0.0, carry_ref[j])
    carry_ref[j] = tail
    return prev8


def _causal_conv3(z, prev8, cw):
    return cw[0:1] * _shift_rows(z, prev8, 2) + cw[1:2] * _shift_rows(z, prev8, 1) + cw[2:3] * z


def _proj_shift_kernel(h_ref, w_ref, mu_ref, o_ref, carry_ref, *, tpb, n_sub, lora_split):
    i, j = pl.program_id(0), pl.program_id(1)
    tn = o_ref.shape[1]
    ts = o_ref.shape[0] // n_sub
    mu = mu_ref[...]
    col = lax.broadcasted_iota(jnp.int32, (1, tn), 1)

    def finish(s, p, prev8):
        out = p + (_shift_rows(p, prev8, 1) - p) * mu
        if lora_split is not None:
            n_tanh, n_lin = lora_split
            out = jnp.where(col < n_tanh, jnp.tanh(out),
                            jnp.where(col < n_tanh + n_lin, out, jax.nn.sigmoid(out)))
        o_ref[s * ts:(s + 1) * ts, :] = out

    prev8 = jnp.where((i % tpb) == 0, 0.0, carry_ref[j])
    pending = None
    for s in range(n_sub):
        p = jnp.dot(h_ref[s * ts:(s + 1) * ts, :w_ref.shape[0]], w_ref[...], preferred_element_type=F32)
        if pending is not None:
            finish(*pending)
        pending = (s, p, prev8)
        prev8 = p[ts - SUBLANES:]
    finish(*pending)
    carry_ref[j] = prev8


def _proj_shift(h, w, mu, rows_per_batch, col0, n, tm_pref, tn, lora_split, name):
    m, dp = h.shape
    d = w.shape[0]
    tm = _tile(rows_per_batch, tm_pref, SUBLANES)
    j0 = col0 // tn
    return pl.pallas_call(
        functools.partial(_proj_shift_kernel, tpb=rows_per_batch // tm, n_sub=tm // _tile(tm, 256, SUBLANES),
                          lora_split=lora_split),
        out_shape=jax.ShapeDtypeStruct((m, n), F32),
        grid=(m // tm, n // tn),
        in_specs=[pl.BlockSpec((tm, dp), lambda i, j: (i, 0)),
                  pl.BlockSpec((d, tn), lambda i, j: (0, j0 + j)),
                  pl.BlockSpec((1, tn), lambda i, j: (0, j0 + j))],
        out_specs=pl.BlockSpec((tm, tn), lambda i, j: (i, j)),
        scratch_shapes=[pltpu.VMEM((n // tn, SUBLANES, tn), F32)],
        compiler_params=_params("arbitrary", "arbitrary"),
        name=name,
    )(h, w, mu)


def _proj_conv_kernel(h_ref, wb_ref, wc_ref, wx_ref, cw_ref, o_ref, carry_ref, *, tpb):
    i, j = pl.program_id(0), pl.program_id(1)
    tm = o_ref.shape[0]
    h = h_ref[:, :wb_ref.shape[0]]
    z = (jnp.dot(h, wc_ref[...], preferred_element_type=F32)
         * jnp.dot(h, wx_ref[...], preferred_element_type=F32))
    prev8 = _carry_swap(carry_ref, j, z[tm - SUBLANES:], (i % tpb) == 0)
    conv = _causal_conv3(z, prev8, cw_ref[...])
    o_ref[...] = (jnp.dot(h, wb_ref[...], preferred_element_type=F32) * conv).astype(o_ref.dtype)


def _proj_conv(h, w, cw, rows_per_batch):
    m, dp = h.shape
    d = w.shape[0]
    dc = cw.shape[1]
    tm = _tile(rows_per_batch, 1024, SUBLANES)
    tc = _tile(dc, 512, LANES)
    ncb = dc // tc
    wspec = lambda grp: pl.BlockSpec((d, tc), lambda i, j: (0, grp * ncb + j))
    return pl.pallas_call(
        functools.partial(_proj_conv_kernel, tpb=rows_per_batch // tm),
        out_shape=jax.ShapeDtypeStruct((m, dc), BF16),
        grid=(m // tm, ncb),
        in_specs=[pl.BlockSpec((tm, dp), lambda i, j: (i, 0)), wspec(0), wspec(1), wspec(2),
                  pl.BlockSpec((CONV_WIDTH, tc), lambda i, j: (0, j))],
        out_specs=pl.BlockSpec((tm, tc), lambda i, j: (i, j)),
        scratch_shapes=[pltpu.VMEM((ncb, SUBLANES, tc), F32)],
        compiler_params=_params("arbitrary", "arbitrary"),
        name="proj_conv",
    )(h, w, w, w, cw)


def _proj_gate_kernel(h_ref, w_ref, o_ref, *, n_sub):
    ts = o_ref.shape[0] // n_sub
    for s in range(n_sub):
        p = jnp.dot(h_ref[s * ts:(s + 1) * ts, :w_ref.shape[0]], w_ref[...], preferred_element_type=F32)
        o_ref[s * ts:(s + 1) * ts, :] = jax.nn.sigmoid(p).astype(o_ref.dtype)


def _proj_gate(h, w, rows_per_batch):
    m, dp = h.shape
    d = w.shape[0]
    n = w.shape[1]
    tm = _tile(rows_per_batch, 2048, SUBLANES)
    tn = _tile(n, 512, LANES)
    return pl.pallas_call(
        functools.partial(_proj_gate_kernel, n_sub=tm // _tile(tm, 256, SUBLANES)),
        out_shape=jax.ShapeDtypeStruct((m, n), BF16),
        grid=(m // tm, n // tn),
        in_specs=[pl.BlockSpec((tm, dp), lambda i, j: (i, 0)),
                  pl.BlockSpec((d, tn), lambda i, j: (0, j))],
        out_specs=pl.BlockSpec((tm, tn), lambda i, j: (i, j)),
        compiler_params=_params("arbitrary", "arbitrary"),
        name="proj_gate",
    )(h, w)


def _sigmoid_tanh(x):
    return 0.5 * jnp.tanh(0.5 * x) + 0.5


def _lora_kernel(a_ref, w_ref, pv_ref, o_ref, *, tiles_per_group):
    grp = pl.program_id(1) // tiles_per_group
    pre = jnp.dot(a_ref[...].astype(BF16), w_ref[...], preferred_element_type=F32) + pv_ref[...]

    @pl.when(grp == 0)
    def _():
        o_ref[...] = -DECAY_SCALE * _sigmoid_tanh(pre)

    @pl.when(grp == 1)
    def _():
        o_ref[...] = _sigmoid_tanh(pre)

    @pl.when(grp == 2)
    def _():
        o_ref[...] = pre


def _lora(ps, w, pv, rows_per_batch, lp, col_block, dr):
    m = ps.shape[0]
    n = w.shape[1]
    tm = _tile(rows_per_batch, 1024, SUBLANES)
    tn = _tile(dr, 2048, LANES)
    return pl.pallas_call(
        functools.partial(_lora_kernel, tiles_per_group=dr // tn),
        out_shape=jax.ShapeDtypeStruct((m, n), F32),
        grid=(m // tm, n // tn),
        in_specs=[pl.BlockSpec((tm, lp), lambda i, j: (i, col_block)),
                  pl.BlockSpec((lp, tn), lambda i, j: (0, j)),
                  pl.BlockSpec((1, tn), lambda i, j: (0, j))],
        out_specs=pl.BlockSpec((tm, tn), lambda i, j: (i, j)),
        compiler_params=_params("arbitrary", "arbitrary"),
        name="lora",
    )(ps, w, pv)


def _split(x, n):
    pieces = []
    for _ in range(n):
        p = x.astype(BF16)
        pieces.append(p)
        x = x - p.astype(F32)
    return pieces


def _dot(a, b, dims, na=1, nb=1):
    pa = [a] if a.dtype == BF16 else _split(a, na)
    pb = [b] if b.dtype == BF16 else _split(b, nb)
    depth = max(len(pa), len(pb))
    acc = None
    for ia, xa in enumerate(pa):
        for ib, xb in enumerate(pb):
            if ia + ib >= depth:
                continue
            d = lax.dot_general(xa, xb, dims, preferred_element_type=F32)
            acc = d if acc is None else acc + d
    return acc


def _scan_kernel(r_ref, k_ref, v_ref, lw_ref, ic_ref, g_ref, kk_ref, ka_ref, rk_ref, lnw_ref, lnb_ref,
                 o_ref, st_ref, *, nb, pairs):
    cl = CHUNK
    pw = 2 * HEAD_DIM

    @pl.when(pl.program_id(1) == 0)
    def _():
        st_ref[...] = jnp.zeros_like(st_ref)

    ri = lax.broadcasted_iota(jnp.int32, (2 * cl, 2 * cl), 0)
    ci = lax.broadcasted_iota(jnp.int32, (2 * cl, 2 * cl), 1)
    strict = ri > ci
    incl = ri >= ci
    eye = jnp.where(ri == ci, 1.0, 0.0).astype(F32)
    hi = lax.broadcasted_iota(jnp.int32, (pw, pw), 0) // HEAD_DIM
    hj = lax.broadcasted_iota(jnp.int32, (pw, pw), 1) // HEAD_DIM
    head_ones = jnp.where(hi == hj, 1.0, 0.0).astype(BF16)
    ti = lax.broadcasted_iota(jnp.int32, (cl, cl), 0)
    tj = lax.broadcasted_iota(jnp.int32, (cl, cl), 1)
    cum_ones = jnp.where(ti >= tj, 1.0, 0.0).astype(BF16)
    first_head = lax.broadcasted_iota(jnp.int32, (1, pw), 1) < HEAD_DIM
    inv_n = 1.0 / HEAD_DIM

    def block_diag(x):
        return jnp.concatenate([jnp.where(first_head, x, 0.0), jnp.where(first_head, 0.0, x)], axis=0)

    def head_sums(xs):
        s = _dot(jnp.concatenate(xs, axis=0), head_ones, _NN)
        return [s[i * cl:(i + 1) * cl] for i in range(len(xs))]

    def each(f, *ls):
        return [f(*xs) for xs in zip(*ls)]

    def rows(*xs):
        return jnp.concatenate(xs, axis=0)

    def cols(*xs):
        return jnp.concatenate(xs, axis=1)

    def prep(grp, q):
        tok = lambda ref: [ref[b, :, pp * pw:(pp + 1) * pw] for b, pp in grp]
        par = lambda ref: [ref[:, pp * pw:(pp + 1) * pw] for _, pp in grp]
        r, kraw, v, lw, ic = tok(r_ref), tok(k_ref), tok(v_ref), tok(lw_ref), tok(ic_ref)
        kk = each(lambda k, p: k * p, kraw, par(kk_ref))
        ss = head_sums(each(lambda x: x * x, kk))
        lg = each(lambda w: _dot(cum_ones, w, _NN, nb=2), lw)
        yield
        kk = each(lambda x, s: x / jnp.maximum(jnp.sqrt(s), 1e-12), kk, ss)
        kmod = each(lambda k, i, p: k * (1.0 + (i - 1.0) * p), kraw, ic, par(ka_ref))
        bvec = each(lambda x, i: x * i, kk, ic)
        yield
        g_inv = each(lambda x: jnp.exp(-x), lg)
        g_end = each(lambda x: jnp.exp(x[cl - 1:cl, :] - x), lg)
        q["gcol"] = each(lambda x: jnp.transpose(jnp.broadcast_to(jnp.exp(x[cl - 1:cl, :]), (pw, pw))), lg)
        yield
        q["a_bd"] = each(lambda x, l, w: block_diag(-x * jnp.exp(l - w)), kk, lg, lw)
        q["r_bd"] = each(lambda x, l: block_diag(x * jnp.exp(l)), r, lg)
        yield
        q["b_bd"] = each(lambda x, gi: block_diag(x * gi), bvec, g_inv)
        q["k_bd"] = each(lambda x, gi: block_diag(x * gi), kmod, g_inv)
        q["v_bd"] = each(block_diag, v)
        yield
        q["bend_bd"] = each(lambda x, ge: block_diag(x * ge), bvec, g_end)
        q["kend_bd"] = each(lambda x, ge: block_diag(x * ge), kmod, g_end)
        q["rkv"] = (r, kmod, v)

    def main(slots, q):
        sc = each(lambda a, rr, b, k: _dot(rows(a, rr), rows(b, k), _NT),
                  q["a_bd"], q["r_bd"], q["b_bd"], q["k_bd"])
        yield
        l_ab = each(lambda s: jnp.where(strict, s[:2 * cl, :2 * cl], 0.0), sc)
        m_ak = each(lambda s: jnp.where(strict, s[:2 * cl, 2 * cl:], 0.0), sc)
        m_rb = each(lambda s: jnp.where(incl, s[2 * cl:, :2 * cl], 0.0), sc)
        m_rk = each(lambda s: jnp.where(incl, s[2 * cl:, 2 * cl:], 0.0), sc)
        tinv = each(lambda l: eye + l, l_ab)
        lpow = each(lambda p: _dot(p, p, _NN), l_ab)
        yield
        for _ in range(cl.bit_length() - 3):
            z = each(lambda p, t: _dot(rows(p, t), p, _NN), lpow, tinv)
            lpow = each(lambda zz: zz[:2 * cl], z)
            tinv = each(lambda t, zz: t + zz[2 * cl:], tinv, z)
            yield
        tinv = each(lambda t, p: t + _dot(t, p, _NN), tinv, lpow)
        yield
        st = [st_ref[i] for i in slots]
        rhs = each(lambda a, mm, s, vv: _dot(cols(a, mm), rows(s, vv), _NN), q["a_bd"], m_ak, st, q["v_bd"])
        yield
        u = each(lambda t, x: _dot(t, x, _NN), tinv, rhs)
        yield
        y_bd = each(lambda rr, mb, mk, s, uu, vv: _dot(cols(rr, mb, mk), rows(s, uu, vv), _NN),
                    q["r_bd"], m_rb, m_rk, st, u, q["v_bd"])
        yield
        for n, i in enumerate(slots):
            st_ref[i] = q["gcol"][n] * st[n] + _dot(rows(q["bend_bd"][n], q["kend_bd"][n]),
                                                    rows(u[n], q["v_bd"][n]), _TN)
        q["y"] = each(lambda x: x[:cl] + x[cl:], y_bd)

    def tail(grp, q):
        par = lambda ref: [ref[:, pp * pw:(pp + 1) * pw] for _, pp in grp]
        r, kmod, v = q["rkv"]
        y = q["y"]
        mean = each(lambda s: s * inv_n, head_sums(y))
        bon = head_sums(each(lambda a, b, p: a * b * p, r, kmod, par(rk_ref)))
        yield
        dy = each(lambda a, b: a - b, y, mean)
        var = each(lambda s: s * inv_n, head_sums(each(lambda x: x * x, dy)))
        yield
        for n, (b, pp) in enumerate(grp):
            sl = slice(pp * pw, (pp + 1) * pw)
            yn = dy[n] * lax.rsqrt(var[n] + LNX_EPS) * lnw_ref[:, sl] + lnb_ref[:, sl]
            o_ref[b, :, sl] = ((yn + bon[n] * v[n]) * g_ref[b, :, sl]).astype(o_ref.dtype)

    def issue(*phases):
        live = list(phases)
        while live:
            for ph in list(live):
                if next(ph, live) is live:
                    live.remove(ph)

    groups = [[(b, pp) for pp in range(pairs)] for b in range(nb)]
    slots = [[b * pairs + pp for pp in range(pairs)] for b in range(nb)]
    state = [dict() for _ in groups]
    issue(prep(groups[0], state[0]))
    for gi in range(len(groups)):
        phases = [main(slots[gi], state[gi])]
        if gi + 1 < len(groups):
            phases.append(prep(groups[gi + 1], state[gi + 1]))
        if gi > 0:
            phases.append(tail(groups[gi - 1], state[gi - 1]))
        issue(*phases)
    issue(tail(groups[-1], state[-1]))


def _scan(ps, lig, k_k, k_a, r_k, lnx_w, lnx_b, nb, seq, dr):
    pw = 2 * HEAD_DIM
    pairs = 16 if dr % (16 * pw) == 0 else 1
    bw = pairs * pw
    npb = dr // bw
    nc = seq // CHUNK
    ps3 = ps.reshape(nb, seq, ps.shape[1])
    lig3 = lig.reshape(nb, seq, lig.shape[1])
    tok = lambda off: pl.BlockSpec((nb, CHUNK, bw), lambda p, c: (0, c, off * npb + p))
    par = pl.BlockSpec((1, bw), lambda p, c: (0, p))
    o = pl.pallas_call(
        functools.partial(_scan_kernel, nb=nb, pairs=pairs),
        out_shape=jax.ShapeDtypeStruct((nb, seq, dr), BF16),
        grid=(npb, nc),
        in_specs=[tok(0), tok(1), tok(2), tok(0), tok(1), tok(2), par, par, par, par, par],
        out_specs=pl.BlockSpec((nb, CHUNK, bw), lambda p, c: (0, c, p)),
        scratch_shapes=[pltpu.VMEM((nb * pairs, pw, pw), F32)],
        compiler_params=_params("arbitrary", "arbitrary"),
        name="rwkv_scan",
    )(ps3, ps3, ps3, lig3, lig3, lig3, k_k, k_a, r_k, lnx_w, lnx_b)
    return o.reshape(nb * seq, dr)


def _merge_kernel(o_ref, y_ref, wa_ref, wb_ref, sa_ref, sb_ref, out_ref):
    ya = jnp.dot(o_ref[...], wa_ref[...], preferred_element_type=F32)
    yb = jnp.dot(y_ref[...], wb_ref[...], preferred_element_type=F32)
    out_ref[...] = (sa_ref[...].astype(F32) * ya + sb_ref[...].astype(F32) * yb).astype(out_ref.dtype)


def _merge(o, ycb, wa, wb, sg, rows_per_batch):
    m, dr = o.shape
    dc = ycb.shape[1]
    d = wa.shape[1]
    tm = _tile(rows_per_batch, 1024, SUBLANES)
    tn = _tile(d, 1024, LANES)
    nj = d // tn
    return pl.pallas_call(
        _merge_kernel,
        out_shape=jax.ShapeDtypeStruct((m, d), BF16),
        grid=(m // tm, nj),
        in_specs=[pl.BlockSpec((tm, dr), lambda i, j: (i, 0)),
                  pl.BlockSpec((tm, dc), lambda i, j: (i, 0)),
                  pl.BlockSpec((dr, tn), lambda i, j: (0, j)),
                  pl.BlockSpec((dc, tn), lambda i, j: (0, j)),
                  pl.BlockSpec((tm, tn), lambda i, j: (i, j)),
                  pl.BlockSpec((tm, tn), lambda i, j: (i, nj + j))],
        out_specs=pl.BlockSpec((tm, tn), lambda i, j: (i, j)),
        compiler_params=_params("arbitrary", "arbitrary"),
        name="merge",
    )(o, ycb, wa, wb, sg, sg)


def _resid_kernel(a_ref, w_ref, x_ref, gt_ref, o_ref):
    y = jnp.dot(a_ref[...], w_ref[...], preferred_element_type=F32)
    o_ref[...] = x_ref[...] + gt_ref[...] * y


def _resid(a, w, x2d, mod3, gate_idx, rows_per_batch, tm_pref, tn_pref, name, cols_outer):
    m, kd = a.shape
    d = w.shape[1]
    tm = _tile(rows_per_batch, tm_pref, SUBLANES)
    tn = _tile(d, tn_pref, LANES)
    tpb = rows_per_batch // tm
    if cols_outer:
        grid = (d // tn, m // tm)
        ij = lambda f: (lambda j, i: f(i, j))
    else:
        grid = (m // tm, d // tn)
        ij = lambda f: f
    return pl.pallas_call(
        _resid_kernel,
        out_shape=jax.ShapeDtypeStruct((m, d), F32),
        grid=grid,
        in_specs=[pl.BlockSpec((tm, kd), ij(lambda i, j: (i, 0))),
                  pl.BlockSpec((kd, tn), ij(lambda i, j: (0, j))),
                  pl.BlockSpec((tm, tn), ij(lambda i, j: (i, j))),
                  pl.BlockSpec((None, 1, tn), ij(lambda i, j: ((i // tpb) * N_ADA + gate_idx, 0, j)))],
        out_specs=pl.BlockSpec((tm, tn), ij(lambda i, j: (i, j))),
        compiler_params=_params("arbitrary", "arbitrary"),
        name=name,
    )(a, w, x2d, mod3)


def _ffn_up_kernel(h_ref, wg_ref, wv_ref, cw_ref, o_ref, carry_ref, *, tpb, n_sub):
    i, j = pl.program_id(0), pl.program_id(1)
    ts = o_ref.shape[0] // n_sub
    cw = cw_ref[...]

    def finish(s, gate, val, prev8):
        gc = _causal_conv3(gate, prev8, cw)
        o_ref[s * ts:(s + 1) * ts, :] = (gc * jax.nn.sigmoid(gc) * val).astype(o_ref.dtype)

    prev8 = jnp.where((i % tpb) == 0, 0.0, carry_ref[j])
    pending = None
    for s in range(n_sub):
        h = h_ref[s * ts:(s + 1) * ts, :wg_ref.shape[0]]
        gate = jnp.dot(h, wg_ref[...], preferred_element_type=F32)
        val = jnp.dot(h, wv_ref[...], preferred_element_type=F32)
        if pending is not None:
            finish(*pending)
        pending = (s, gate, val, prev8)
        prev8 = gate[ts - SUBLANES:]
    finish(*pending)
    carry_ref[j] = prev8


def _ffn_up(h, w, cw, rows_per_batch):
    m, dp = h.shape
    d = w.shape[0]
    dff = cw.shape[1]
    tm = _tile(rows_per_batch, 2048, SUBLANES)
    tc = _tile(dff, 256, LANES)
    ncb = dff // tc
    return pl.pallas_call(
        functools.partial(_ffn_up_kernel, tpb=rows_per_batch // tm, n_sub=tm // _tile(tm, 128, SUBLANES)),
        out_shape=jax.ShapeDtypeStruct((m, dff), BF16),
        grid=(m // tm, ncb),
        in_specs=[pl.BlockSpec((tm, dp), lambda i, j: (i, 0)),
                  pl.BlockSpec((d, tc), lambda i, j: (0, j)),
                  pl.BlockSpec((d, tc), lambda i, j: (0, ncb + j)),
                  pl.BlockSpec((CONV_WIDTH, tc), lambda i, j: (0, j))],
        out_specs=pl.BlockSpec((tm, tc), lambda i, j: (i, j)),
        scratch_shapes=[pltpu.VMEM((ncb, SUBLANES, tc), F32)],
        compiler_params=_params("arbitrary", "arbitrary"),
        name="ffn_up",
    )(h, w, w, cw)


def kernel(x, c, w_ada, b_ada, norm1_gain, w_in, mu_shift, w0, a0, k_k, k_a, r_k, w_lora_decay,
           w_lora_iclr, w_lora_gate, lnx_w, lnx_b, conv_w_mix, w_o_rwkv, w_o_conv, w_out, norm2_gain,
           w_ffn_up, conv_w_ffn, w_ffn_down, final_gain):
    nb, seq, d = x.shape
    m = nb * seq
    depth = w_ada.shape[0]
    dr = w_o_rwkv.shape[1]
    dc = w_o_conv.shape[1]
    n_dec, n_icl, n_gat = w_lora_decay.shape[1], w_lora_iclr.shape[1], w_lora_gate.shape[1]
    n_lora = n_dec + n_icl + n_gat
    n_shift = 3 * dr + n_lora
    assert seq % CHUNK == 0 and dr % (2 * HEAD_DIM) == 0

    lp = _round_up(n_lora, _tile(dr, 512, LANES))
    assert (3 * dr) % lp == 0 and w_in.shape[2] >= 3 * dr + lp

    x2d = x.reshape(m, d)
    for layer in range(depth):
        wi = w_in[layer].astype(BF16)
        mu_a = jnp.pad(mu_shift[layer], (0, 3 * dr + lp - n_shift)).reshape(1, -1)
        w_c = wi[:, n_shift:n_shift + 3 * dc]
        w_g = wi[:, n_shift + 3 * dc:]
        w_l = jnp.zeros((lp, 3 * dr), F32)
        w_l = w_l.at[:n_dec, :dr].set(w_lora_decay[layer])
        w_l = w_l.at[n_dec:n_dec + n_icl, dr:2 * dr].set(w_lora_iclr[layer])
        w_l = w_l.at[n_dec + n_icl:n_lora, 2 * dr:].set(w_lora_gate[layer]).astype(BF16)
        pv_l = jnp.concatenate([w0[layer], a0[layer], jnp.zeros((dr,), F32)]).reshape(1, 3 * dr)
        row = lambda p: p.reshape(1, dr)

        mod3 = _ada(c, w_ada[layer], b_ada[layer]).reshape(nb * N_ADA, 1, d)
        h = _norm_mod(x2d, norm1_gain[layer], mod3, 0, 1, seq)
        ps = _proj_shift(h, wi, mu_a, seq, 0, 3 * dr, 2048, _tile(dr, 512, LANES), None, "proj_rkv")
        la = _proj_shift(h, wi, mu_a, seq, 3 * dr, lp, 1024, lp, (n_dec, n_icl), "proj_lora")
        ycb = _proj_conv(h, w_c, conv_w_mix[layer], seq)
        sg = _proj_gate(h, w_g, seq)
        lig = _lora(la, w_l, pv_l, seq, lp, 0, dr)
        o = _scan(ps, lig, row(k_k[layer]), row(k_a[layer]), row(r_k[layer]), row(lnx_w[layer]),
                  row(lnx_b[layer]), nb, seq, dr)
        merged = _merge(o, ycb, w_o_rwkv[layer].astype(BF16), w_o_conv[layer].astype(BF16), sg, seq)
        x1 = _resid(merged, w_out[layer].astype(BF16), x2d, mod3, 2, seq, 1024, 1024, "attn_out", False)
        h2 = _norm_mod(x1, norm2_gain[layer], mod3, 3, 4, seq)
        act = _ffn_up(h2, w_ffn_up[layer].astype(BF16), conv_w_ffn[layer], seq)
        x2d = _resid(act, w_ffn_down[layer].astype(BF16), x1, mod3, 5, seq, 512, 512, "ffn_down", True)
    return _final_norm(x2d, final_gain).reshape(nb, seq, d)
```

```python
import functools
import math

import jax
import jax.numpy as jnp
from jax import lax
from jax.experimental import pallas as pl
from jax.experimental.pallas import tpu as pltpu

F32 = jnp.float32
BF16 = jnp.bfloat16

NORM_EPS = 1e-6
LNX_EPS = 64e-5
DECAY_SCALE = math.exp(-0.5)
HEAD_DIM = 64
N_ADA = 6
CONV_WIDTH = 3
LANES = 128
SUBLANES = 8
CHUNK = 64
VMEM_LIMIT = 56 * 1024 * 1024
LHS_PAD = LANES

_NN = (((1,), (0,)), ((), ()))
_NT = (((1,), (1,)), ((), ()))
_TN = (((0,), (0,)), ((), ()))


def _round_up(n, m):
    return (n + m - 1) // m * m


def _tile(n, pref, quantum):
    t = min(pref, n) // quantum * quantum
    while t > quantum and n % t:
        t -= quantum
    assert t > 0 and n % t == 0, (n, pref, quantum)
    return t


def _params(*sem):
    return pltpu.CompilerParams(dimension_semantics=sem, vmem_limit_bytes=VMEM_LIMIT)


def _ada_kernel(cb_ref, w_ref, b_ref, o_ref, act_ref, *, nb, kd, tn):
    rep = tn // LANES

    @pl.when(pl.program_id(0) == 0)
    def _():
        cv = cb_ref[...]
        act_ref[...] = cv * jax.nn.sigmoid(cv)

    def body(kc, accs):
        k0 = pl.multiple_of(kc * SUBLANES, SUBLANES)
        w = w_ref[pl.ds(k0, SUBLANES), :]
        return tuple(accs[b] + w * jnp.concatenate([act_ref[b, pl.ds(k0, SUBLANES), :]] * rep, axis=1)
                     for b in range(nb))

    init = tuple(jnp.zeros((SUBLANES, tn), F32) for _ in range(nb))
    accs = lax.fori_loop(0, kd // SUBLANES, body, init, unroll=8)
    rows = [jnp.sum(a, axis=0, keepdims=True) for a in accs]
    o_ref[...] = jnp.concatenate(rows, axis=0) + b_ref[...]


def _ada(c, w, b):
    nb, kd = c.shape
    n = w.shape[1]
    tn = _tile(n, 1024, LANES)
    cb = jnp.broadcast_to(c[:, :, None], (nb, kd, LANES))
    return pl.pallas_call(
        functools.partial(_ada_kernel, nb=nb, kd=kd, tn=tn),
        out_shape=jax.ShapeDtypeStruct((nb, n), F32),
        grid=(n // tn,),
        in_specs=[pl.BlockSpec((nb, kd, LANES), lambda j: (0, 0, 0)),
                  pl.BlockSpec((kd, tn), lambda j: (0, j)),
                  pl.BlockSpec((1, tn), lambda j: (0, j))],
        out_specs=pl.BlockSpec((nb, tn), lambda j: (0, j)),
        scratch_shapes=[pltpu.VMEM((nb, kd, LANES), F32)],
        compiler_params=_params("arbitrary"),
        name="ada",
    )(cb, w, b.reshape(1, n))


def _rms(x):
    return x * lax.rsqrt(jnp.mean(x * x, axis=-1, keepdims=True) + NORM_EPS)


def _norm_mod_kernel(x_ref, g_ref, sh_ref, sc_ref, o_ref):
    y = _rms(x_ref[...]) * g_ref[...]
    d = x_ref.shape[1]
    o_ref[:, :d] = (y * (1.0 + sc_ref[...]) + sh_ref[...]).astype(o_ref.dtype)
    o_ref[:, d:] = jnp.zeros((o_ref.shape[0], o_ref.shape[1] - d), o_ref.dtype)


def _norm_mod(x2d, gain, mod3, shift_idx, scale_idx, rows_per_batch):
    m, d = x2d.shape
    tm = _tile(rows_per_batch, 512, SUBLANES)
    tpb = rows_per_batch // tm
    return pl.pallas_call(
        _norm_mod_kernel,
        out_shape=jax.ShapeDtypeStruct((m, d + LHS_PAD), BF16),
        grid=(m // tm,),
        in_specs=[pl.BlockSpec((tm, d), lambda i: (i, 0)),
                  pl.BlockSpec((1, d), lambda i: (0, 0)),
                  pl.BlockSpec((None, 1, d), lambda i: ((i // tpb) * N_ADA + shift_idx, 0, 0)),
                  pl.BlockSpec((None, 1, d), lambda i: ((i // tpb) * N_ADA + scale_idx, 0, 0))],
        out_specs=pl.BlockSpec((tm, d + LHS_PAD), lambda i: (i, 0)),
        compiler_params=_params("arbitrary"),
        name="norm_mod",
    )(x2d, gain.reshape(1, d), mod3, mod3)


def _final_norm_kernel(x_ref, g_ref, o_ref):
    o_ref[...] = _rms(x_ref[...]) * g_ref[...]


def _final_norm(x2d, gain):
    m, d = x2d.shape
    tm = _tile(m, 512, SUBLANES)
    return pl.pallas_call(
        _final_norm_kernel,
        out_shape=jax.ShapeDtypeStruct((m, d), F32),
        grid=(m // tm,),
        in_specs=[pl.BlockSpec((tm, d), lambda i: (i, 0)),
                  pl.BlockSpec((1, d), lambda i: (0, 0))],
        out_specs=pl.BlockSpec((tm, d), lambda i: (i, 0)),
        compiler_params=_params("arbitrary"),
        name="final_norm",
    )(x2d, gain.reshape(1, d))


def _shift_rows(p, prev8, s):
    rolled = pltpu.roll(p, s, 0)
    row = lax.broadcasted_iota(jnp.int32, prev8.shape, 0)
    head = jnp.where(row < s, pltpu.roll(prev8, s, 0), rolled[:SUBLANES])
    return jnp.concatenate([head, rolled[SUBLANES:]], axis=0)


def _carry_swap(carry_ref, j, tail, first):
    prev8 = jnp.where(first, 0.0, carry_ref[j])
    carry_ref[j] = tail
    return prev8


def _causal_conv3(z, prev8, cw):
    return cw[0:1] * _shift_rows(z, prev8, 2) + cw[1:2] * _shift_rows(z, prev8, 1) + cw[2:3] * z


def _proj_shift_kernel(h_ref, w_ref, mu_ref, o_ref, carry_ref, *, tpb, n_sub, lora_split):
    i, j = pl.program_id(0), pl.program_id(1)
    tn = o_ref.shape[1]
    ts = o_ref.shape[0] // n_sub
    mu = mu_ref[...]
    col = lax.broadcasted_iota(jnp.int32, (1, tn), 1)

    def finish(s, p, prev8):
        out = p + (_shift_rows(p, prev8, 1) - p) * mu
        if lora_split is not None:
            n_tanh, n_lin = lora_split
            out = jnp.where(col < n_tanh, jnp.tanh(out),
                            jnp.where(col < n_tanh + n_lin, out, jax.nn.sigmoid(out)))
        o_ref[s * ts:(s + 1) * ts, :] = out

    prev8 = jnp.where((i % tpb) == 0, 0.0, carry_ref[j])
    pending = None
    for s in range(n_sub):
        p = jnp.dot(h_ref[s * ts:(s + 1) * ts, :w_ref.shape[0]], w_ref[...], preferred_element_type=F32)
        if pending is not None:
            finish(*pending)
        pending = (s, p, prev8)
        prev8 = p[ts - SUBLANES:]
    finish(*pending)
    carry_ref[j] = prev8


def _proj_shift(h, w, mu, rows_per_batch, col0, n, tm_pref, tn, lora_split, name):
    m, dp = h.shape
    d = w.shape[0]
    tm = _tile(rows_per_batch, tm_pref, SUBLANES)
    j0 = col0 // tn
    return pl.pallas_call(
        functools.partial(_proj_shift_kernel, tpb=rows_per_batch // tm, n_sub=tm // _tile(tm, 256, SUBLANES),
                          lora_split=lora_split),
        out_shape=jax.ShapeDtypeStruct((m, n), F32),
        grid=(m // tm, n // tn),
        in_specs=[pl.BlockSpec((tm, dp), lambda i, j: (i, 0)),
                  pl.BlockSpec((d, tn), lambda i, j: (0, j0 + j)),
                  pl.BlockSpec((1, tn), lambda i, j: (0, j0 + j))],
        out_specs=pl.BlockSpec((tm, tn), lambda i, j: (i, j)),
        scratch_shapes=[pltpu.VMEM((n // tn, SUBLANES, tn), F32)],
        compiler_params=_params("arbitrary", "arbitrary"),
        name=name,
    )(h, w, mu)


def _proj_conv_kernel(h_ref, wb_ref, wc_ref, wx_ref, cw_ref, o_ref, carry_ref, *, tpb):
    i, j = pl.program_id(0), pl.program_id(1)
    tm = o_ref.shape[0]
    h = h_ref[:, :wb_ref.shape[0]]
    z = (jnp.dot(h, wc_ref[...], preferred_element_type=F32)
         * jnp.dot(h, wx_ref[...], preferred_element_type=F32))
    prev8 = _carry_swap(carry_ref, j, z[tm - SUBLANES:], (i % tpb) == 0)
    conv = _causal_conv3(z, prev8, cw_ref[...])
    o_ref[...] = (jnp.dot(h, wb_ref[...], preferred_element_type=F32) * conv).astype(o_ref.dtype)


def _proj_conv(h, w, cw, rows_per_batch):
    m, dp = h.shape
    d = w.shape[0]
    dc = cw.shape[1]
    tm = _tile(rows_per_batch, 1024, SUBLANES)
    tc = _tile(dc, 512, LANES)
    ncb = dc // tc
    wspec = lambda grp: pl.BlockSpec((d, tc), lambda i, j: (0, grp * ncb + j))
    return pl.pallas_call(
        functools.partial(_proj_conv_kernel, tpb=rows_per_batch // tm),
        out_shape=jax.ShapeDtypeStruct((m, dc), BF16),
        grid=(m // tm, ncb),
        in_specs=[pl.BlockSpec((tm, dp), lambda i, j: (i, 0)), wspec(0), wspec(1), wspec(2),
                  pl.BlockSpec((CONV_WIDTH, tc), lambda i, j: (0, j))],
        out_specs=pl.BlockSpec((tm, tc), lambda i, j: (i, j)),
        scratch_shapes=[pltpu.VMEM((ncb, SUBLANES, tc), F32)],
        compiler_params=_params("arbitrary", "arbitrary"),
        name="proj_conv",
    )(h, w, w, w, cw)


def _proj_gate_kernel(h_ref, w_ref, o_ref, *, n_sub):
    ts = o_ref.shape[0] // n_sub
    for s in range(n_sub):
        p = jnp.dot(h_ref[s * ts:(s + 1) * ts, :w_ref.shape[0]], w_ref[...], preferred_element_type=F32)
        o_ref[s * ts:(s + 1) * ts, :] = jax.nn.sigmoid(p).astype(o_ref.dtype)


def _proj_gate(h, w, rows_per_batch):
    m, dp = h.shape
    d = w.shape[0]
    n = w.shape[1]
    tm = _tile(rows_per_batch, 2048, SUBLANES)
    tn = _tile(n, 512, LANES)
    return pl.pallas_call(
        functools.partial(_proj_gate_kernel, n_sub=tm // _tile(tm, 256, SUBLANES)),
        out_shape=jax.ShapeDtypeStruct((m, n), BF16),
        grid=(m // tm, n // tn),
        in_specs=[pl.BlockSpec((tm, dp), lambda i, j: (i, 0)),
                  pl.BlockSpec((d, tn), lambda i, j: (0, j))],
        out_specs=pl.BlockSpec((tm, tn), lambda i, j: (i, j)),
        compiler_params=_params("arbitrary", "arbitrary"),
        name="proj_gate",
    )(h, w)


def _sigmoid_tanh(x):
    return 0.5 * jnp.tanh(0.5 * x) + 0.5


def _lora_kernel(a_ref, w_ref, pv_ref, o_ref, *, tiles_per_group):
    grp = pl.program_id(1) // tiles_per_group
    pre = jnp.dot(a_ref[...].astype(BF16), w_ref[...], preferred_element_type=F32) + pv_ref[...]

    @pl.when(grp == 0)
    def _():
        o_ref[...] = -DECAY_SCALE * _sigmoid_tanh(pre)

    @pl.when(grp == 1)
    def _():
        o_ref[...] = _sigmoid_tanh(pre)

    @pl.when(grp == 2)
    def _():
        o_ref[...] = pre


def _lora(ps, w, pv, rows_per_batch, lp, col_block, dr):
    m = ps.shape[0]
    n = w.shape[1]
    tm = _tile(rows_per_batch, 1024, SUBLANES)
    tn = _tile(dr, 2048, LANES)
    return pl.pallas_call(
        functools.partial(_lora_kernel, tiles_per_group=dr // tn),
        out_shape=jax.ShapeDtypeStruct((m, n), F32),
        grid=(m // tm, n // tn),
        in_specs=[pl.BlockSpec((tm, lp), lambda i, j: (i, col_block)),
                  pl.BlockSpec((lp, tn), lambda i, j: (0, j)),
                  pl.BlockSpec((1, tn), lambda i, j: (0, j))],
        out_specs=pl.BlockSpec((tm, tn), lambda i, j: (i, j)),
        compiler_params=_params("arbitrary", "arbitrary"),
        name="lora",
    )(ps, w, pv)


def _split(x, n):
    pieces = []
    for _ in range(n):
        p = x.astype(BF16)
        pieces.append(p)
        x = x - p.astype(F32)
    return pieces


def _dot(a, b, dims, na=1, nb=1):
    pa = [a] if a.dtype == BF16 else _split(a, na)
    pb = [b] if b.dtype == BF16 else _split(b, nb)
    depth = max(len(pa), len(pb))
    acc = None
    for ia, xa in enumerate(pa):
        for ib, xb in enumerate(pb):
            if ia + ib >= depth:
                continue
            d = lax.dot_general(xa, xb, dims, preferred_element_type=F32)
            acc = d if acc is None else acc + d
    return acc


def _scan_kernel(r_ref, k_ref, v_ref, lw_ref, ic_ref, g_ref, kk_ref, ka_ref, rk_ref, lnw_ref, lnb_ref,
                 o_ref, st_ref, *, nb, pairs):
    cl = CHUNK
    pw = 2 * HEAD_DIM

    @pl.when(pl.program_id(1) == 0)
    def _():
        st_ref[...] = jnp.zeros_like(st_ref)

    ri = lax.broadcasted_iota(jnp.int32, (2 * cl, 2 * cl), 0)
    ci = lax.broadcasted_iota(jnp.int32, (2 * cl, 2 * cl), 1)
    strict = ri > ci
    incl = ri >= ci
    eye = jnp.where(ri == ci, 1.0, 0.0).astype(F32)
    hi = lax.broadcasted_iota(jnp.int32, (pw, pw), 0) // HEAD_DIM
    hj = lax.broadcasted_iota(jnp.int32, (pw, pw), 1) // HEAD_DIM
    head_ones = jnp.where(hi == hj, 1.0, 0.0).astype(BF16)
    ti = lax.broadcasted_iota(jnp.int32, (cl, cl), 0)
    tj = lax.broadcasted_iota(jnp.int32, (cl, cl), 1)
    cum_ones = jnp.where(ti >= tj, 1.0, 0.0).astype(BF16)
    first_head = lax.broadcasted_iota(jnp.int32, (1, pw), 1) < HEAD_DIM
    inv_n = 1.0 / HEAD_DIM

    def block_diag(x):
        return jnp.concatenate([jnp.where(first_head, x, 0.0), jnp.where(first_head, 0.0, x)], axis=0)

    def head_sums(xs):
        s = _dot(jnp.concatenate(xs, axis=0), head_ones, _NN)
        return [s[i * cl:(i + 1) * cl] for i in range(len(xs))]

    def each(f, *ls):
        return [f(*xs) for xs in zip(*ls)]

    def rows(*xs):
        return jnp.concatenate(xs, axis=0)

    def cols(*xs):
        return jnp.concatenate(xs, axis=1)

    def prep(grp, q):
        tok = lambda ref: [ref[b, :, pp * pw:(pp + 1) * pw] for b, pp in grp]
        par = lambda ref: [ref[:, pp * pw:(pp + 1) * pw] for _, pp in grp]
        r, kraw, v, lw, ic = tok(r_ref), tok(k_ref), tok(v_ref), tok(lw_ref), tok(ic_ref)
        kk = each(lambda k, p: k * p, kraw, par(kk_ref))
        ss = head_sums(each(lambda x: x * x, kk))
        lg = each(lambda w: _dot(cum_ones, w, _NN, nb=2), lw)
        yield
        kk = each(lambda x, s: x / jnp.maximum(jnp.sqrt(s), 1e-12), kk, ss)
        kmod = each(lambda k, i, p: k * (1.0 + (i - 1.0) * p), kraw, ic, par(ka_ref))
        bvec = each(lambda x, i: x * i, kk, ic)
        yield
        g_inv = each(lambda x: jnp.exp(-x), lg)
        g_end = each(lambda x: jnp.exp(x[cl - 1:cl, :] - x), lg)
        q["gcol"] = each(lambda x: jnp.transpose(jnp.broadcast_to(jnp.exp(x[cl - 1:cl, :]), (pw, pw))), lg)
        yield
        q["a_bd"] = each(lambda x, l, w: block_diag(-x * jnp.exp(l - w)), kk, lg, lw)
        q["r_bd"] = each(lambda x, l: block_diag(x * jnp.exp(l)), r, lg)
        yield
        q["b_bd"] = each(lambda x, gi: block_diag(x * gi), bvec, g_inv)
        q["k_bd"] = each(lambda x, gi: block_diag(x * gi), kmod, g_inv)
        q["v_bd"] = each(block_diag, v)
        yield
        q["bend_bd"] = each(lambda x, ge: block_diag(x * ge), bvec, g_end)
        q["kend_bd"] = each(lambda x, ge: block_diag(x * ge), kmod, g_end)
        q["rkv"] = (r, kmod, v)

    def main(slots, q):
        sc = each(lambda a, rr, b, k: _dot(rows(a, rr), rows(b, k), _NT),
                  q["a_bd"], q["r_bd"], q["b_bd"], q["k_bd"])
        yield
        l_ab = each(lambda s: jnp.where(strict, s[:2 * cl, :2 * cl], 0.0), sc)
        m_ak = each(lambda s: jnp.where(strict, s[:2 * cl, 2 * cl:], 0.0), sc)
        m_rb = each(lambda s: jnp.where(incl, s[2 * cl:, :2 * cl], 0.0), sc)
        m_rk = each(lambda s: jnp.where(incl, s[2 * cl:, 2 * cl:], 0.0), sc)
        tinv = each(lambda l: eye + l, l_ab)
        lpow = each(lambda p: _dot(p, p, _NN), l_ab)
        yield
        for _ in range(cl.bit_length() - 3):
            z = each(lambda p, t: _dot(rows(p, t), p, _NN), lpow, tinv)
            lpow = each(lambda zz: zz[:2 * cl], z)
            tinv = each(lambda t, zz: t + zz[2 * cl:], tinv, z)
            yield
        tinv = each(lambda t, p: t + _dot(t, p, _NN), tinv, lpow)
        yield
        st = [st_ref[i] for i in slots]
        rhs = each(lambda a, mm, s, vv: _dot(cols(a, mm), rows(s, vv), _NN), q["a_bd"], m_ak, st, q["v_bd"])
        yield
        u = each(lambda t, x: _dot(t, x, _NN), tinv, rhs)
        yield
        y_bd = each(lambda rr, mb, mk, s, uu, vv: _dot(cols(rr, mb, mk), rows(s, uu, vv), _NN),
                    q["r_bd"], m_rb, m_rk, st, u, q["v_bd"])
        yield
        for n, i in enumerate(slots):
            st_ref[i] = q["gcol"][n] * st[n] + _dot(rows(q["bend_bd"][n], q["kend_bd"][n]),
                                                    rows(u[n], q["v_bd"][n]), _TN)
        q["y"] = each(lambda x: x[:cl] + x[cl:], y_bd)

    def tail(grp, q):
        par = lambda ref: [ref[:, pp * pw:(pp + 1) * pw] for _, pp in grp]
        r, kmod, v = q["rkv"]
        y = q["y"]
        mean = each(lambda s: s * inv_n, head_sums(y))
        bon = head_sums(each(lambda a, b, p: a * b * p, r, kmod, par(rk_ref)))
        yield
        dy = each(lambda a, b: a - b, y, mean)
        var = each(lambda s: s * inv_n, head_sums(each(lambda x: x * x, dy)))
        yield
        for n, (b, pp) in enumerate(grp):
            sl = slice(pp * pw, (pp + 1) * pw)
            yn = dy[n] * lax.rsqrt(var[n] + LNX_EPS) * lnw_ref[:, sl] + lnb_ref[:, sl]
            o_ref[b, :, sl] = ((yn + bon[n] * v[n]) * g_ref[b, :, sl]).astype(o_ref.dtype)

    def issue(*phases):
        live = list(phases)
        while live:
            for ph in list(live):
                if next(ph, live) is live:
                    live.remove(ph)

    groups = [[(b, pp) for pp in range(pairs)] for b in range(nb)]
    slots = [[b * pairs + pp for pp in range(pairs)] for b in range(nb)]
    state = [dict() for _ in groups]
    issue(prep(groups[0], state[0]))
    for gi in range(len(groups)):
        phases = [main(slots[gi], state[gi])]
        if gi + 1 < len(groups):
            phases.append(prep(groups[gi + 1], state[gi + 1]))
        if gi > 0:
            phases.append(tail(groups[gi - 1], state[gi - 1]))
        issue(*phases)
    issue(tail(groups[-1], state[-1]))


def _scan(ps, lig, k_k, k_a, r_k, lnx_w, lnx_b, nb, seq, dr):
    pw = 2 * HEAD_DIM
    pairs = 16 if dr % (16 * pw) == 0 else 1
    bw = pairs * pw
    npb = dr // bw
    nc = seq // CHUNK
    ps3 = ps.reshape(nb, seq, ps.shape[1])
    lig3 = lig.reshape(nb, seq, lig.shape[1])
    tok = lambda off: pl.BlockSpec((nb, CHUNK, bw), lambda p, c: (0, c, off * npb + p))
    par = pl.BlockSpec((1, bw), lambda p, c: (0, p))
    o = pl.pallas_call(
        functools.partial(_scan_kernel, nb=nb, pairs=pairs),
        out_shape=jax.ShapeDtypeStruct((nb, seq, dr), BF16),
        grid=(npb, nc),
        in_specs=[tok(0), tok(1), tok(2), tok(0), tok(1), tok(2), par, par, par, par, par],
        out_specs=pl.BlockSpec((nb, CHUNK, bw), lambda p, c: (0, c, p)),
        scratch_shapes=[pltpu.VMEM((nb * pairs, pw, pw), F32)],
        compiler_params=_params("arbitrary", "arbitrary"),
        name="rwkv_scan",
    )(ps3, ps3, ps3, lig3, lig3, lig3, k_k, k_a, r_k, lnx_w, lnx_b)
    return o.reshape(nb * seq, dr)


def _merge_kernel(o_ref, y_ref, wa_ref, wb_ref, sa_ref, sb_ref, out_ref):
    ya = jnp.dot(o_ref[...], wa_ref[...], preferred_element_type=F32)
    yb = jnp.dot(y_ref[...], wb_ref[...], preferred_element_type=F32)
    out_ref[...] = (sa_ref[...].astype(F32) * ya + sb_ref[...].astype(F32) * yb).astype(out_ref.dtype)


def _merge(o, ycb, wa, wb, sg, rows_per_batch):
    m, dr = o.shape
    dc = ycb.shape[1]
    d = wa.shape[1]
    tm = _tile(rows_per_batch, 1024, SUBLANES)
    tn = _tile(d, 1024, LANES)
    nj = d // tn
    return pl.pallas_call(
        _merge_kernel,
        out_shape=jax.ShapeDtypeStruct((m, d), BF16),
        grid=(m // tm, nj),
        in_specs=[pl.BlockSpec((tm, dr), lambda i, j: (i, 0)),
                  pl.BlockSpec((tm, dc), lambda i, j: (i, 0)),
                  pl.BlockSpec((dr, tn), lambda i, j: (0, j)),
                  pl.BlockSpec((dc, tn), lambda i, j: (0, j)),
                  pl.BlockSpec((tm, tn), lambda i, j: (i, j)),
                  pl.BlockSpec((tm, tn), lambda i, j: (i, nj + j))],
        out_specs=pl.BlockSpec((tm, tn), lambda i, j: (i, j)),
        compiler_params=_params("arbitrary", "arbitrary"),
        name="merge",
    )(o, ycb, wa, wb, sg, sg)


def _resid_kernel(a_ref, w_ref, x_ref, gt_ref, o_ref):
    y = jnp.dot(a_ref[...], w_ref[...], preferred_element_type=F32)
    o_ref[...] = x_ref[...] + gt_ref[...] * y


def _resid(a, w, x2d, mod3, gate_idx, rows_per_batch, tm_pref, tn_pref, name, cols_outer):
    m, kd = a.shape
    d = w.shape[1]
    tm = _tile(rows_per_batch, tm_pref, SUBLANES)
    tn = _tile(d, tn_pref, LANES)
    tpb = rows_per_batch // tm
    if cols_outer:
        grid = (d // tn, m // tm)
        ij = lambda f: (lambda j, i: f(i, j))
    else:
        grid = (m // tm, d // tn)
        ij = lambda f: f
    return pl.pallas_call(
        _resid_kernel,
        out_shape=jax.ShapeDtypeStruct((m, d), F32),
        grid=grid,
        in_specs=[pl.BlockSpec((tm, kd), ij(lambda i, j: (i, 0))),
                  pl.BlockSpec((kd, tn), ij(lambda i, j: (0, j))),
                  pl.BlockSpec((tm, tn), ij(lambda i, j: (i, j))),
                  pl.BlockSpec((None, 1, tn), ij(lambda i, j: ((i // tpb) * N_ADA + gate_idx, 0, j)))],
        out_specs=pl.BlockSpec((tm, tn), ij(lambda i, j: (i, j))),
        compiler_params=_params("arbitrary", "arbitrary"),
        name=name,
    )(a, w, x2d, mod3)


def _ffn_up_kernel(h_ref, wg_ref, wv_ref, cw_ref, o_ref, carry_ref, *, tpb, n_sub):
    i, j = pl.program_id(0), pl.program_id(1)
    ts = o_ref.shape[0] // n_sub
    cw = cw_ref[...]

    def finish(s, gate, val, prev8):
        gc = _causal_conv3(gate, prev8, cw)
        o_ref[s * ts:(s + 1) * ts, :] = (gc * jax.nn.sigmoid(gc) * val).astype(o_ref.dtype)

    prev8 = jnp.where((i % tpb) == 0, 0.0, carry_ref[j])
    pending = None
    for s in range(n_sub):
        h = h_ref[s * ts:(s + 1) * ts, :wg_ref.shape[0]]
        gate = jnp.dot(h, wg_ref[...], preferred_element_type=F32)
        val = jnp.dot(h, wv_ref[...], preferred_element_type=F32)
        if pending is not None:
            finish(*pending)
        pending = (s, gate, val, prev8)
        prev8 = gate[ts - SUBLANES:]
    finish(*pending)
    carry_ref[j] = prev8


def _ffn_up(h, w, cw, rows_per_batch):
    m, dp = h.shape
    d = w.shape[0]
    dff = cw.shape[1]
    tm = _tile(rows_per_batch, 2048, SUBLANES)
    tc = _tile(dff, 256, LANES)
    ncb = dff // tc
    return pl.pallas_call(
        functools.partial(_ffn_up_kernel, tpb=rows_per_batch // tm, n_sub=tm // _tile(tm, 128, SUBLANES)),
        out_shape=jax.ShapeDtypeStruct((m, dff), BF16),
        grid=(m // tm, ncb),
        in_specs=[pl.BlockSpec((tm, dp), lambda i, j: (i, 0)),
                  pl.BlockSpec((d, tc), lambda i, j: (0, j)),
                  pl.BlockSpec((d, tc), lambda i, j: (0, ncb + j)),
                  pl.BlockSpec((CONV_WIDTH, tc), lambda i, j: (0, j))],
        out_specs=pl.BlockSpec((tm, tc), lambda i, j: (i, j)),
        scratch_shapes=[pltpu.VMEM((ncb, SUBLANES, tc), F32)],
        compiler_params=_params("arbitrary", "arbitrary"),
        name="ffn_up",
    )(h, w, w, cw)


def kernel(x, c, w_ada, b_ada, norm1_gain, w_in, mu_shift, w0, a0, k_k, k_a, r_k, w_lora_decay,
           w_lora_iclr, w_lora_gate, lnx_w, lnx_b, conv_w_mix, w_o_rwkv, w_o_conv, w_out, norm2_gain,
           w_ffn_up, conv_w_ffn, w_ffn_down, final_gain):
    nb, seq, d = x.shape
    m = nb * seq
    depth = w_ada.shape[0]
    dr = w_o_rwkv.shape[1]
    dc = w_o_conv.shape[1]
    n_dec, n_icl, n_gat = w_lora_decay.shape[1], w_lora_iclr.shape[1], w_lora_gate.shape[1]
    n_lora = n_dec + n_icl + n_gat
    n_shift = 3 * dr + n_lora
    assert seq % CHUNK == 0 and dr % (2 * HEAD_DIM) == 0

    lp = _round_up(n_lora, _tile(dr, 512, LANES))
    assert (3 * dr) % lp == 0 and w_in.shape[2] >= 3 * dr + lp

    x2d = x.reshape(m, d)
    for layer in range(depth):
        wi = w_in[layer].astype(BF16)
        mu_a = jnp.pad(mu_shift[layer], (0, 3 * dr + lp - n_shift)).reshape(1, -1)
        w_c = wi[:, n_shift:n_shift + 3 * dc]
        w_g = wi[:, n_shift + 3 * dc:]
        w_l = jnp.zeros((lp, 3 * dr), F32)
        w_l = w_l.at[:n_dec, :dr].set(w_lora_decay[layer])
        w_l = w_l.at[n_dec:n_dec + n_icl, dr:2 * dr].set(w_lora_iclr[layer])
        w_l = w_l.at[n_dec + n_icl:n_lora, 2 * dr:].set(w_lora_gate[layer]).astype(BF16)
        pv_l = jnp.concatenate([w0[layer], a0[layer], jnp.zeros((dr,), F32)]).reshape(1, 3 * dr)
        row = lambda p: p.reshape(1, dr)

        mod3 = _ada(c, w_ada[layer], b_ada[layer]).reshape(nb * N_ADA, 1, d)
        h = _norm_mod(x2d, norm1_gain[layer], mod3, 0, 1, seq)
        ps = _proj_shift(h, wi, mu_a, seq, 0, 3 * dr, 2048, _tile(dr, 512, LANES), None, "proj_rkv")
        la = _proj_shift(h, wi, mu_a, seq, 3 * dr, lp, 1024, lp, (n_dec, n_icl), "proj_lora")
        ycb = _proj_conv(h, w_c, conv_w_mix[layer], seq)
        sg = _proj_gate(h, w_g, seq)
        lig = _lora(la, w_l, pv_l, seq, lp, 0, dr)
        o = _scan(ps, lig, row(k_k[layer]), row(k_a[layer]), row(r_k[layer]), row(lnx_w[layer]),
                  row(lnx_b[layer]), nb, seq, dr)
        merged = _merge(o, ycb, w_o_rwkv[layer].astype(BF16), w_o_conv[layer].astype(BF16), sg, seq)
        x1 = _resid(merged, w_out[layer].astype(BF16), x2d, mod3, 2, seq, 1024, 1024, "attn_out", False)
        h2 = _norm_mod(x1, norm2_gain[layer], mod3, 3, 4, seq)
        act = _ffn_up(h2, w_ffn_up[layer].astype(BF16), conv_w_ffn[layer], seq)
        x2d = _resid(act, w_ffn_down[layer].astype(BF16), x1, mod3, 5, seq, 512, 512, "ffn_down", True)
    return _final_norm(x2d, final_gain).reshape(nb, seq, d)
```
